```python
import math
import jax, jax.numpy as jnp
from jax import lax
import numpy as np

D_MODEL = 1024
BATCH = 8
SEQ = 2048
DEPTH = 2
DEC_BATCH = 32
DEC_SEQ = 1
PAST_LEN = 16384
PAGE_SIZE = 128

N_META = 16
EPS = 1e-6
A_HEADS = 4
A_DK = 64
A_DV = 64
B_HEADS = 4
B_DK = 64
B_DV = 2 * B_DK
ROT_DIM = B_DK // 4
ROPE_THETA = 500000.0
Q_BLOCK = 128
C_HEADS = 4
C_DK = 64
C_DV = 64
CONV_W = 4
C_CONV = C_HEADS * (2 * C_DK + C_DV)
CHUNK = 16
D_FF = 4 * D_MODEL
MIX_W = A_HEADS * A_DV + B_HEADS * B_DV + C_HEADS * C_DV
SPLIT_SIZES = (A_HEADS * A_DK, A_HEADS * A_DK, A_HEADS * A_DV, A_HEADS * A_DV,
               B_HEADS * 2 * B_DK, B_HEADS * 2 * B_DK, B_HEADS * B_DV,
               C_CONV, C_HEADS * C_DV, C_HEADS, C_HEADS)
D_IN = sum(SPLIT_SIZES)

kernel_name = 'hymba_hgrn2_diffattn_gdn_decoder_step'


def _rmsnorm(x, g):
    xf = x.astype(jnp.float32)
    y = xf * lax.rsqrt(jnp.mean(xf * xf, axis=-1, keepdims=True) + EPS)
    return (y * g.astype(jnp.float32)).astype(x.dtype)


def _l2norm(x):
    xf = x.astype(jnp.float32)
    return xf * lax.rsqrt(jnp.sum(xf * xf, axis=-1, keepdims=True) + EPS)


def _heads(x, h):
    return x.reshape(x.shape[:-1] + (h, x.shape[-1] // h))


def _partial_rope(x, pos):
    half = ROT_DIM // 2
    inv = jnp.power(ROPE_THETA, -jnp.arange(half, dtype=jnp.float32) * 2.0 / ROT_DIM)
    ang = pos.astype(jnp.float32)[:, None] * inv[None, :]
    cos = jnp.cos(ang)[:, None, None, :]
    sin = jnp.sin(ang)[:, None, None, :]
    x1 = x[..., :half].astype(jnp.float32)
    x2 = x[..., half:ROT_DIM].astype(jnp.float32)
    rot = jnp.concatenate([x1 * cos - x2 * sin, x2 * cos + x1 * sin], axis=-1).astype(x.dtype)
    return jnp.concatenate([rot, x[..., ROT_DIM:]], axis=-1)


def _causal_dwconv(x_ext, w):
    rhs = w[:, None, :].astype(x_ext.dtype)
    return lax.conv_general_dilated(x_ext, rhs, window_strides=(1,), padding='VALID',
                                    dimension_numbers=('NWC', 'WIO', 'NWC'),
                                    feature_group_count=x_ext.shape[-1])


def _to_chunks(a):
    b, t = a.shape[:2]
    a = a.reshape((b, t // CHUNK, CHUNK) + a.shape[2:])
    return jnp.moveaxis(a, 2, 3)


def _from_chunks(a):
    a = jnp.moveaxis(a, 3, 2)
    return a.reshape((a.shape[0], a.shape[1] * a.shape[2]) + a.shape[3:])


def _hgrn2_chunked(q, k, v, g):
    q, k, v, g = (_to_chunks(a) for a in (q, k, v, g))
    b, n, h, c, dk = q.shape
    dv = v.shape[-1]
    G = jnp.cumsum(g, axis=3)
    Gl = G[..., -1, :]
    causal = jnp.tril(jnp.ones((CHUNK, CHUNK), bool))
    rel = jnp.exp(jnp.where(causal[:, :, None], G[..., :, None, :] - G[..., None, :, :], -jnp.inf))
    att = jnp.einsum('bnhtd,bnhsd,bnhtsd->bnhts', q, k, rel)
    o = jnp.einsum('bnhts,bnhsv->bnhtv', att, v)
    kv = jnp.einsum('bnhsd,bnhsv->bnhdv', k * jnp.exp(Gl[..., None, :] - G), v)

    def step(S, inp):
        glc, kvc = inp
        return jnp.exp(glc)[..., None] * S + kvc, S

    S0 = jnp.zeros((b, h, dk, dv), jnp.float32)
    S_fin, S_start = lax.scan(step, S0, (jnp.moveaxis(Gl, 1, 0), jnp.moveaxis(kv, 1, 0)))
    S_start = jnp.moveaxis(S_start, 0, 1)
    o = o + jnp.einsum('bnhtd,bnhdv->bnhtv', q * jnp.exp(G), S_start)
    return _from_chunks(o), S_fin


def _hgrn2_steps(S, q, k, v, g):
    def step(S, inp):
        qt, kt, vt, gt = inp
        S = jnp.exp(gt)[..., None] * S + kt[..., :, None] * vt[..., None, :]
        return S, jnp.einsum('bhd,bhdv->bhv', qt, S)

    S, o = lax.scan(step, S.astype(jnp.float32), tuple(jnp.moveaxis(a, 1, 0) for a in (q, k, v, g)))
    return jnp.moveaxis(o, 0, 1), S


def _delta_chunked(q, k, v, beta, loga):
    q, k, v = (_to_chunks(a) for a in (q, k, v))
    beta, loga = _to_chunks(beta), _to_chunks(loga)
    b, n, h, c, dk = q.shape
    dv = v.shape[-1]
    G = jnp.cumsum(loga, axis=-1)
    Gl = G[..., -1]
    causal = jnp.tril(jnp.ones((CHUNK, CHUNK), bool))
    strict = jnp.tril(jnp.ones((CHUNK, CHUNK), bool), k=-1)
    rel = jnp.exp(jnp.where(causal, G[..., :, None] - G[..., None, :], -jnp.inf))
    kk = jnp.einsum('bnhtd,bnhsd->bnhts', k, k)
    a = jnp.where(strict, beta[..., :, None] * rel * kk, 0.0)
    lower = jnp.eye(CHUNK, dtype=jnp.float32) + a
    rhs = jnp.concatenate([(beta * jnp.exp(G))[..., None] * k, beta[..., None] * v], axis=-1)
    sol = lax.linalg.triangular_solve(lower, rhs, left_side=True, lower=True, unit_diagonal=True)
    w, u0 = sol[..., :dk], sol[..., dk:]
    qk = rel * jnp.einsum('bnhtd,bnhsd->bnhts', q, k)
    kd = k * jnp.exp(Gl[..., None] - G)[..., None]

    def step(S, inp):
        wc, uc, kdc, glc = inp
        u = uc - jnp.einsum('bhtd,bhdv->bhtv', wc, S)
        S_new = jnp.exp(glc)[..., None, None] * S + jnp.einsum('bhtd,bhtv->bhdv', kdc, u)
        return S_new, S

    S0 = jnp.zeros((b, h, dk, dv), jnp.float32)
    S_fin, S_start = lax.scan(step, S0, tuple(jnp.moveaxis(x, 1, 0) for x in (w, u0, kd, Gl)))
    S_start = jnp.moveaxis(S_start, 0, 1)
    u = u0 - jnp.einsum('bnhtd,bnhdv->bnhtv', w, S_start)
    o = (jnp.einsum('bnhtd,bnhdv->bnhtv', q * jnp.exp(G)[..., None], S_start)
         + jnp.einsum('bnhts,bnhsv->bnhtv', qk, u))
    return _from_chunks(o), S_fin


def _delta_steps(S, q, k, v, beta, loga):
    def step(S, inp):
        qt, kt, vt, bt, at = inp
        S = jnp.exp(at)[..., None, None] * S
        u = bt[..., None] * (vt - jnp.einsum('bhd,bhdv->bhv', kt, S))
        S = S + kt[..., :, None] * u[..., None, :]
        return S, jnp.einsum('bhd,bhdv->bhv', qt, S)

    S, o = lax.scan(step, S.astype(jnp.float32),
                    tuple(jnp.moveaxis(a, 1, 0) for a in (q, k, v, beta, loga)))
    return jnp.moveaxis(o, 0, 1), S


def _diff_probs(s, mask, lam):
    p = jax.nn.softmax(jnp.where(mask, s, -jnp.inf), axis=-1)
    return p[:, 0] - lam * p[:, 1]


def _diff_attn_prompt(q, k, v, lam):
    b, L, h = q.shape[:3]
    nb = -(-L // Q_BLOCK)
    Lp = nb * Q_BLOCK
    qb = jnp.pad(q, ((0, 0), (0, Lp - L), (0, 0), (0, 0), (0, 0)))
    qb = jnp.moveaxis(qb.reshape(b, nb, Q_BLOCK, h, 2, B_DK), 1, 0)
    kpos = jnp.arange(L)
    vf = v.astype(jnp.float32)
    scale = B_DK ** -0.5

    def block(args):
        qblk, i = args
        qpos = i * Q_BLOCK + jnp.arange(Q_BLOCK)
        s = jnp.einsum('bqhcd,bkhcd->bchqk', qblk, k).astype(jnp.float32) * scale
        p = _diff_probs(s, kpos[None, :] <= qpos[:, None], lam)
        return jnp.einsum('bhqk,bkhe->bqhe', p, vf)

    o = lax.map(block, (qb, jnp.arange(nb)))
    return jnp.moveaxis(o, 0, 1).reshape(b, Lp, h, B_DV)[:, :L]


def _diff_attn_sample(q, k_new, v_new, k_past, v_past, lam):
    t = q.shape[1]
    p_len = k_past.shape[1]
    scale = B_DK ** -0.5
    s = jnp.concatenate([jnp.einsum('bqhcd,bkhcd->bchqk', q, k_past),
                         jnp.einsum('bqhcd,bkhcd->bchqk', q, k_new)], axis=-1).astype(jnp.float32) * scale
    mask = jnp.concatenate([jnp.ones((t, p_len), bool), jnp.tril(jnp.ones((t, t), bool))], axis=-1)
    p = _diff_probs(s, mask, lam)
    return (jnp.einsum('bhqk,bkhe->bqhe', p[..., :p_len], v_past.astype(jnp.float32))
            + jnp.einsum('bhqk,bkhe->bqhe', p[..., p_len:], v_new.astype(jnp.float32)))


def _prep_mixers(x, conv_ctx, pos, lb, g_mix, w_in_l, gq, gk, conv_w_l, a_log, dt_bias):
    f32 = jnp.float32
    idx = [int(s) for s in np.cumsum(SPLIT_SIZES)[:-1]]
    (aq, af, ai, ag, bq, bk, bv, cqkv, cg, cb, ca) = jnp.split(_rmsnorm(x, g_mix) @ w_in_l, idx, axis=-1)
    fg = lb + (1.0 - lb) * jax.nn.sigmoid(af.astype(f32))
    hg = (_heads(jax.nn.silu(aq.astype(f32)), A_HEADS), _heads(1.0 - fg, A_HEADS),
          _heads(ai.astype(f32), A_HEADS), _heads(jnp.log(fg), A_HEADS))
    bsh = x.shape[:2] + (B_HEADS, 2, B_DK)
    qb = _partial_rope(_rmsnorm(bq.reshape(bsh), gq), pos)
    kb = _partial_rope(_rmsnorm(bk.reshape(bsh), gk), pos)
    vb = _heads(bv, B_HEADS)
    cx = jnp.concatenate([conv_ctx.astype(x.dtype), cqkv], axis=1)
    act = jax.nn.silu(_causal_dwconv(cx, conv_w_l).astype(f32))
    qc, kc, vc = jnp.split(act, [C_HEADS * C_DK, 2 * C_HEADS * C_DK], axis=-1)
    dl = (_l2norm(_heads(qc, C_HEADS)) * (C_DK ** -0.5), _l2norm(_heads(kc, C_HEADS)), _heads(vc, C_HEADS),
          jax.nn.sigmoid(cb.astype(f32)),
          -jnp.exp(a_log.astype(f32)) * jax.nn.softplus(ca.astype(f32) + dt_bias.astype(f32)))
    return hg, (qb, kb, vb), dl, ag, cg, cx[:, -(CONV_W - 1):]


def _merge_and_ffn(x, oa, ag, ob, oc, cg, b_scale, g_a, g_b, g_c, w_o, g_ffn, w_u, w_d):
    f32 = jnp.float32
    lead = x.shape[:2]
    ya = (_rmsnorm(oa, g_a) * jax.nn.silu(_heads(ag.astype(f32), A_HEADS))).reshape(lead + (-1,))
    yb = (_rmsnorm(ob, g_b).astype(f32) * b_scale).reshape(lead + (-1,))
    yc = (_rmsnorm(oc, g_c) * jax.nn.silu(_heads(cg.astype(f32), C_HEADS))).reshape(lead + (-1,))
    x = x + jnp.concatenate([ya, yb, yc], axis=-1).astype(x.dtype) @ w_o
    h = _rmsnorm(x, g_ffn)
    return x + jnp.square(jax.nn.relu(h @ w_u)) @ w_d


def setup_inputs(seed: int = 0) -> dict:
    key = jax.random.key(seed)
    ks = jax.random.split(key, 28)
    f32 = jnp.float32
    n_pages = PAST_LEN // PAGE_SIZE
    n_used = DEC_BATCH * n_pages
    n_pool = n_used + n_used // 4

    def nrm(k, shape, s=1.0):
        return s * jax.random.normal(k, shape, f32)

    dt = jnp.exp(jax.random.uniform(ks[22], (DEPTH, C_HEADS), f32, math.log(1e-3), math.log(1e-1)))
    return {
        'x_prompt': nrm(ks[0], (BATCH, SEQ, D_MODEL)),
        'x_sample': nrm(ks[1], (DEC_BATCH, DEC_SEQ, D_MODEL)),
        'cache_k': nrm(ks[2], (DEPTH, n_pool, PAGE_SIZE, B_HEADS, 2 * B_DK)),
        'cache_v': nrm(ks[3], (DEPTH, n_pool, PAGE_SIZE, B_HEADS, B_DV)),
        'page_table': jax.random.permutation(ks[4], n_pool)[:n_used].reshape(DEC_BATCH, n_pages).astype(jnp.int32),
        'state_hgrn': nrm(ks[5], (DEPTH, DEC_BATCH, A_HEADS, A_DK, A_DV), 0.5),
        'state_delta': nrm(ks[6], (DEPTH, DEC_BATCH, C_HEADS, C_DK, C_DV), 0.2),
        'state_conv': nrm(ks[7], (DEPTH, DEC_BATCH, CONV_W - 1, C_CONV)),
        'meta_tokens': nrm(ks[8], (N_META, D_MODEL)),
        'norm_mix': 1.0 + nrm(ks[9], (DEPTH, D_MODEL), 0.02),
        'norm_ffn': 1.0 + nrm(ks[10], (DEPTH, D_MODEL), 0.02),
        'w_in': nrm(ks[11], (DEPTH, D_MODEL, D_IN), D_MODEL ** -0.5),
        'w_out': nrm(ks[12], (DEPTH, MIX_W, D_MODEL), MIX_W ** -0.5),
        'hgrn_lower_bound': nrm(ks[13], (DEPTH, A_HEADS * A_DK), 0.1),
        'hgrn_out_norm': 1.0 + nrm(ks[14], (DEPTH, A_DV), 0.02),
        'diff_q_norm': 1.0 + nrm(ks[15], (DEPTH, B_DK), 0.02),
        'diff_k_norm': 1.0 + nrm(ks[16], (DEPTH, B_DK), 0.02),
        'diff_lambda': nrm(ks[17], (DEPTH, 4, B_DK), 0.1),
        'diff_out_norm': 1.0 + nrm(ks[18], (DEPTH, B_DV), 0.02),
        'conv_w': nrm(ks[19], (DEPTH, CONV_W, C_CONV), CONV_W ** -0.5),
        'delta_a_log': jnp.log(jax.random.uniform(ks[20], (DEPTH, C_HEADS), f32, 1.0, 16.0)),
        'delta_dt_bias': dt + jnp.log(-jnp.expm1(-dt)),
        'delta_out_norm': 1.0 + nrm(ks[23], (DEPTH, C_DV), 0.02),
        'w_up': nrm(ks[24], (DEPTH, D_MODEL, D_FF), D_MODEL ** -0.5),
        'w_down': nrm(ks[25], (DEPTH, D_FF, D_MODEL), D_FF ** -0.5),
    }


def reference(x_prompt, x_sample, cache_k, cache_v, page_table, state_hgrn, state_delta, state_conv,
              meta_tokens, norm_mix, norm_ffn, w_in, w_out, hgrn_lower_bound, hgrn_out_norm,
              diff_q_norm, diff_k_norm, diff_lambda, diff_out_norm, conv_w, delta_a_log,
              delta_dt_bias, delta_out_norm, w_up, w_down):
    f32 = jnp.float32
    nb_p = x_prompt.shape[0]
    nb_s = x_sample.shape[0]
    xp = jnp.concatenate([jnp.broadcast_to(meta_tokens[None].astype(x_prompt.dtype), (nb_p, N_META, D_MODEL)),
                          x_prompt], axis=1)
    xs = x_sample
    n_past = page_table.shape[1] * cache_k.shape[2]
    pos_p = jnp.arange(xp.shape[1])
    pos_s = n_past + jnp.arange(xs.shape[1])
    lb_soft = jax.nn.softmax(hgrn_lower_bound.astype(f32), axis=0)
    lbs = jnp.cumsum(lb_soft, axis=0) - lb_soft[0]

    kp_l, vp_l, ks_l, vs_l = [], [], [], []
    hp_l, hs_l, dp_l, ds_l, cp_l, cs_l = [], [], [], [], [], []
    for l in range(DEPTH):
        lam_init = 0.8 - 0.6 * math.exp(-0.3 * l)
        lv = diff_lambda[l].astype(f32)
        lam = jnp.exp(jnp.sum(lv[0] * lv[1])) - jnp.exp(jnp.sum(lv[2] * lv[3])) + lam_init
        b_scale = 1.0 - lam_init

        ctx0 = jnp.zeros((nb_p, CONV_W - 1, C_CONV), xp.dtype)
        hg, (qb, kb, vb), dl, ag, cg, conv_new = _prep_mixers(
            xp, ctx0, pos_p, lbs[l], norm_mix[l], w_in[l], diff_q_norm[l], diff_k_norm[l],
            conv_w[l], delta_a_log[l], delta_dt_bias[l])
        oa, sa = _hgrn2_chunked(*hg)
        ob = _diff_attn_prompt(qb, kb, vb, lam)
        oc, sc = _delta_chunked(*dl)
        xp = _merge_and_ffn(xp, oa, ag, ob, oc, cg, b_scale, hgrn_out_norm[l], diff_out_norm[l],
                            delta_out_norm[l], w_out[l], norm_ffn[l], w_up[l], w_down[l])
        kp_l.append(kb.reshape(kb.shape[:3] + (2 * B_DK,)).astype(cache_k.dtype))
        vp_l.append(vb.astype(cache_v.dtype))
        hp_l.append(sa.astype(state_hgrn.dtype))
        dp_l.append(sc.astype(state_delta.dtype))
        cp_l.append(conv_new.astype(state_conv.dtype))

        hg, (qb, kb, vb), dl, ag, cg, conv_new = _prep_mixers(
            xs, state_conv[l], pos_s, lbs[l], norm_mix[l], w_in[l], diff_q_norm[l], diff_k_norm[l],
            conv_w[l], delta_a_log[l], delta_dt_bias[l])
        oa, sa = _hgrn2_steps(state_hgrn[l], *hg)
        k_past = cache_k[l, page_table].reshape(nb_s, n_past, B_HEADS, 2, B_DK)
        v_past = cache_v[l, page_table].reshape(nb_s, n_past, B_HEADS, B_DV)
        ob = _diff_attn_sample(qb, kb, vb, k_past, v_past, lam)
        oc, sc = _delta_steps(state_delta[l], *dl)
        xs = _merge_and_ffn(xs, oa, ag, ob, oc, cg, b_scale, hgrn_out_norm[l], diff_out_norm[l],
                            delta_out_norm[l], w_out[l], norm_ffn[l], w_up[l], w_down[l])
        ks_l.append(kb.reshape(kb.shape[:3] + (2 * B_DK,)).astype(cache_k.dtype))
        vs_l.append(vb.astype(cache_v.dtype))
        hs_l.append(sa.astype(state_hgrn.dtype))
        ds_l.append(sc.astype(state_delta.dtype))
        cs_l.append(conv_new.astype(state_conv.dtype))

    y_prompt = xp[:, N_META:]
    y_sample = xs
    return (y_prompt, y_sample,
            jnp.stack(kp_l), jnp.stack(vp_l), jnp.stack(ks_l), jnp.stack(vs_l),
            jnp.stack(hp_l), jnp.stack(hs_l), jnp.stack(dp_l), jnp.stack(ds_l),
            jnp.stack(cp_l), jnp.stack(cs_l))
```

```python
import functools
import math

import jax
import jax.numpy as jnp
from jax import lax
from jax.experimental import pallas as pl
from jax.experimental.pallas import tpu as pltpu

F32 = jnp.float32
BF16 = jnp.bfloat16

EPS = 1e-6
N_META = 16
CHUNK = 16
BLK = 128
HEADS = 4
HD = 64
HW = HEADS * HD
B_DV = 128
ROT_DIM = 16
ROPE_THETA = 500000.0
CONV_W = 4
C_CONV = 3 * HW
D_IN = 3592
NZ = 3712
OFF_CQKV, OFF_CG = 0, 768
OFF_A = 1024
OFF_BQ, OFF_BK, OFF_BV = 2048, 2560, 3072
OFF_CS = 3584

VMEM_LIMIT = 48 * 1024 * 1024


def _bdot(a, b):
    return jnp.dot(a.astype(BF16), b.astype(BF16), preferred_element_type=F32)


def _bdot_nt(a, b):
    return lax.dot_general(a.astype(BF16), b.astype(BF16), (((1,), (1,)), ((), ())),
                           preferred_element_type=F32)


def _bdot_tn(a, b):
    return lax.dot_general(a.astype(BF16), b.astype(BF16), (((0,), (0,)), ((), ())),
                           preferred_element_type=F32)


def _split(a, n):
    parts, r = [], a
    for i in range(n):
        p = r.astype(BF16)
        parts.append(p)
        if i + 1 < n:
            r = r - p.astype(F32)
    return parts


def _xdot_l(m01, a, n=3):
    return sum(jnp.dot(m01, p, preferred_element_type=F32) for p in _split(a, n))


def _xdot_r(a, m01, n=3):
    return sum(jnp.dot(p, m01, preferred_element_type=F32) for p in _split(a, n))


def _xdot_nt(m01, a, n=3):
    return sum(lax.dot_general(m01, p, (((1,), (1,)), ((), ())), preferred_element_type=F32)
               for p in _split(a, n))


def _dot3(a, b):
    ah, al = _split(a, 2)
    bh, bl = _split(b, 2)
    return (jnp.dot(ah, bh, preferred_element_type=F32)
            + jnp.dot(ah, bl, preferred_element_type=F32)
            + jnp.dot(al, bh, preferred_element_type=F32))


def _sigmoid(x):
    return 1.0 / (1.0 + jnp.exp(-x))


def _silu(x):
    return x * _sigmoid(x)


def _softplus(x):
    return jnp.maximum(x, 0.0) + jnp.log(1.0 + jnp.exp(-jnp.abs(x)))


def _iota(shape, dim):
    return lax.broadcasted_iota(jnp.int32, shape, dim)


def _div(x, n):
    return lax.shift_right_logical(x, int(math.log2(n)))


def _chunk_masks(n):
    r, c = _iota((n, n), 0), _iota((n, n), 1)
    same = _div(r, CHUNK) == _div(c, CHUNK)
    return same, same & (c <= r), same & (c < r)


def _as01(mask):
    return jnp.where(mask, 1.0, 0.0).astype(BF16)


def _lower_bound(p, layer):
    mx = jnp.max(p, axis=0, keepdims=True)
    ex = jnp.exp(p - mx)
    soft = ex / jnp.sum(ex, axis=0, keepdims=True)
    if layer == 0:
        return jnp.zeros_like(soft[0:1])
    return jnp.sum(soft[1:layer + 1], axis=0, keepdims=True)


def _lambda(lv, lam_init):
    a = jnp.sum(lv[0:1] * lv[1:2], axis=1, keepdims=True)
    b = jnp.sum(lv[2:3] * lv[3:4], axis=1, keepdims=True)
    return jnp.exp(a) - jnp.exp(b) + lam_init


def _params(*sem):
    return pltpu.CompilerParams(dimension_semantics=sem, vmem_limit_bytes=VMEM_LIMIT)


def _proj_kernel(x_ref, g_ref, w_ref, o_ref):
    x = x_ref[...]
    h = x * lax.rsqrt(jnp.mean(x * x, axis=-1, keepdims=True) + EPS) * g_ref[...]
    o_ref[...] = jnp.dot(h.astype(BF16), w_ref[...], preferred_element_type=F32)


def _proj(x, g, w, tm):
    m, d = x.shape
    n = w.shape[1]
    return pl.pallas_call(
        _proj_kernel,
        grid=(m // tm,),
        in_specs=[pl.BlockSpec((tm, d), lambda i: (i, 0)),
                  pl.BlockSpec((1, d), lambda i: (0, 0)),
                  pl.BlockSpec((d, n), lambda i: (0, 0))],
        out_specs=pl.BlockSpec((tm, n), lambda i: (i, 0)),
        out_shape=jax.ShapeDtypeStruct((m, n), F32),
        compiler_params=_params("parallel"),
        name="proj",
    )(x, g, w)


def _ffn_kernel(x_ref, ya_ref, yb_ref, yc_ref, wo_ref, g_ref, wu_ref, wd_ref, o_ref, h_sc, acc_sc):
    k = pl.program_id(1)

    @pl.when(k == 0)
    def _():
        mix = jnp.concatenate([ya_ref[...], yb_ref[...], yc_ref[...]], axis=-1)
        x1 = x_ref[...] + jnp.dot(mix.astype(BF16), wo_ref[...], preferred_element_type=F32)
        h = x1 * lax.rsqrt(jnp.mean(x1 * x1, axis=-1, keepdims=True) + EPS) * g_ref[...]
        h_sc[...] = h.astype(BF16)
        acc_sc[...] = x1

    hid = jnp.dot(h_sc[...], wu_ref[...], preferred_element_type=F32)
    hid = jnp.square(jnp.maximum(hid, 0.0))
    acc_sc[...] += jnp.dot(hid.astype(BF16), wd_ref[...], preferred_element_type=F32)

    @pl.when(k == pl.num_programs(1) - 1)
    def _():
        o_ref[...] = acc_sc[...]


def _merge_ffn(x, ya, yb, yc, wo, g, wu, wd, tm, tf):
    m, d = x.shape
    dff = wu.shape[1]
    row = lambda i, k: (i, 0)
    return pl.pallas_call(
        _ffn_kernel,
        grid=(m // tm, dff // tf),
        in_specs=[pl.BlockSpec((tm, d), row),
                  pl.BlockSpec((tm, ya.shape[1]), row),
                  pl.BlockSpec((tm, yb.shape[1]), row),
                  pl.BlockSpec((tm, yc.shape[1]), row),
                  pl.BlockSpec(wo.shape, lambda i, k: (0, 0)),
                  pl.BlockSpec((1, d), lambda i, k: (0, 0)),
                  pl.BlockSpec((d, tf), lambda i, k: (0, k)),
                  pl.BlockSpec((tf, d), lambda i, k: (k, 0))],
        out_specs=pl.BlockSpec((tm, d), row),
        out_shape=jax.ShapeDtypeStruct((m, d), F32),
        scratch_shapes=[pltpu.VMEM((tm, d), BF16), pltpu.VMEM((tm, d), F32)],
        compiler_params=_params("parallel", "arbitrary"),
        name="merge_ffn",
    )(x, ya, yb, yc, wo, g, wu, wd)


def _qkrope_kernel(gq_ref, gk_ref, e_ref, c_ref, s1_ref, s2_ref, q_ref, k_ref, qo_ref, ko_ref):
    e = e_ref[...]
    cos, s_next, s_prev = c_ref[...], s1_ref[...], s2_ref[...]

    def norm_rope(x, g, scale):
        ss = _xdot_r(x * x, e, 2)
        xn = x * lax.rsqrt(ss * (1.0 / HD) + EPS) * g
        outs = []
        for h in range(HEADS):
            s = xn[:, h * 128:(h + 1) * 128]
            r = (s * cos + pltpu.roll(s, 128 - ROT_DIM // 2, 1) * s_next
                 + pltpu.roll(s, ROT_DIM // 2, 1) * s_prev)
            outs.append(r * scale if scale != 1.0 else r)
        return jnp.concatenate(outs, axis=1)

    qo_ref[0] = norm_rope(q_ref[0], gq_ref[...], HD ** -0.5)
    ko_ref[0] = norm_rope(k_ref[0], gk_ref[...], 1.0)


def _qkrope(z3, gq, gk, e512, tabs, tt):
    b, t, _ = z3.shape
    w = 2 * HW
    const = lambda shape: pl.BlockSpec(shape, lambda i, j: (0,) * len(shape))
    tab = pl.BlockSpec((tt, 128), lambda i, j: (j, 0))
    out = pl.BlockSpec((1, tt, w), lambda i, j: (i, j, 0))
    return pl.pallas_call(
        _qkrope_kernel,
        grid=(b, t // tt),
        in_specs=[const((1, w)), const((1, w)), const((w, w)), tab, tab, tab,
                  pl.BlockSpec((1, tt, w), lambda i, j: (i, j, OFF_BQ // w)),
                  pl.BlockSpec((1, tt, w), lambda i, j: (i, j, OFF_BK // w))],
        out_specs=[out, out],
        out_shape=[jax.ShapeDtypeStruct((b, t, w), F32)] * 2,
        compiler_params=_params("parallel", "parallel"),
        name="qkrope",
    )(gq, gk, e512, *tabs, z3, z3)


def _rope_tables(pos):
    half = ROT_DIM // 2
    inv = jnp.power(ROPE_THETA, -jnp.arange(half, dtype=F32) * 2.0 / ROT_DIM)
    ang = pos.astype(F32)[:, None] * inv[None, :]
    cos, sin = jnp.cos(ang), jnp.sin(ang)
    n = pos.shape[0]
    one = jnp.ones((n, HD - ROT_DIM), F32)
    zero = jnp.zeros((n, HD - half), F32)
    c = jnp.concatenate([cos, cos, one], axis=1)
    s_next = jnp.concatenate([-sin, zero], axis=1)
    s_prev = jnp.concatenate([jnp.zeros((n, half), F32), sin, one * 0.0], axis=1)
    return tuple(jnp.tile(a, (1, 2)) for a in (c, s_next, s_prev))


def _attn_kernel(lam_ref, gb_ref, q_ref, k_ref, v_ref, o_ref, *, lam_init, b_scale):
    i = pl.program_id(2)
    q = q_ref[0]
    lane = _iota((1, 128), 1)
    q1 = jnp.where(lane < HD, q, 0.0).astype(BF16)
    q2 = jnp.where(lane >= HD, q, 0.0).astype(BF16)
    row = _iota((BLK, BLK), 0)
    col = _iota((BLK, BLK), 1)

    def body(j, carry):
        m1, l1, a1, m2, l2, a2 = carry
        kb = k_ref[0, pl.ds(pl.multiple_of(j * BLK, BLK), BLK), :].astype(BF16)
        vb = v_ref[0, pl.ds(pl.multiple_of(j * BLK, BLK), BLK), :].astype(BF16)
        keep = (j * BLK + col) <= (i * BLK + row)

        def upd(qc, m, l, a):
            s = lax.dot_general(qc, kb, (((1,), (1,)), ((), ())), preferred_element_type=F32)
            s = jnp.where(keep, s, -jnp.inf)
            mn = jnp.maximum(m, jnp.max(s, axis=1, keepdims=True))
            al = jnp.exp(m - mn)
            p = jnp.exp(s - mn)
            l = al * l + jnp.sum(p, axis=1, keepdims=True)
            a = al * a + jnp.dot(p.astype(BF16), vb, preferred_element_type=F32)
            return mn, l, a

        m1, l1, a1 = upd(q1, m1, l1, a1)
        m2, l2, a2 = upd(q2, m2, l2, a2)
        return m1, l1, a1, m2, l2, a2

    neg = jnp.full((BLK, 1), -jnp.inf, F32)
    zl = jnp.zeros((BLK, 1), F32)
    za = jnp.zeros((BLK, B_DV), F32)
    m1, l1, a1, m2, l2, a2 = lax.fori_loop(0, i + 1, body, (neg, zl, za, neg, zl, za))
    lam = _lambda(lam_ref[...], lam_init)
    o = a1 / l1 - lam * (a2 / l2)
    y = o * lax.rsqrt(jnp.mean(o * o, axis=-1, keepdims=True) + EPS) * gb_ref[...]
    o_ref[0] = y * b_scale


def _attn_prompt(lam_p, gb, qn, kn, z3, lam_init, b_scale):
    b, t, _ = qn.shape
    kern = functools.partial(_attn_kernel, lam_init=lam_init, b_scale=b_scale)
    return pl.pallas_call(
        kern,
        grid=(b, HEADS, t // BLK),
        in_specs=[pl.BlockSpec(lam_p.shape, lambda i, h, j: (0, 0)),
                  pl.BlockSpec((1, B_DV), lambda i, h, j: (0, 0)),
                  pl.BlockSpec((1, BLK, 128), lambda i, h, j: (i, j, h)),
                  pl.BlockSpec((1, t, 128), lambda i, h, j: (i, 0, h)),
                  pl.BlockSpec((1, t, B_DV), lambda i, h, j: (i, 0, OFF_BV // B_DV + h))],
        out_specs=pl.BlockSpec((1, BLK, B_DV), lambda i, h, j: (i, j, h)),
        out_shape=jax.ShapeDtypeStruct((b, t, HEADS * B_DV), F32),
        compiler_params=_params("parallel", "parallel", "arbitrary"),
        name="attn_prompt",
    )(lam_p, gb, qn, kn, z3)


def _hgrn_kernel(lb_ref, gn_ref, e_ref, z_ref, y_ref, st_ref, s_sc, *, layer, l_true):
    tb = pl.program_id(1)

    @pl.when(tb == 0)
    def _():
        s_sc[...] = jnp.zeros_like(s_sc)

    z = z_ref[0]
    aq, af = z[:, 0:HW], z[:, HW:2 * HW]
    v, ag = z[:, 2 * HW:3 * HW], z[:, 3 * HW:4 * HW]
    lb = _lower_bound(lb_ref[...], layer)
    f = lb + (1.0 - lb) * _sigmoid(af)
    valid = (tb * BLK + _iota((BLK, 1), 0)) < l_true
    k = jnp.where(valid, 1.0 - f, 0.0)
    g = jnp.where(valid, jnp.log(f), 0.0)
    q = _silu(aq)

    same, tri, _ = _chunk_masks(BLK)
    gc = _xdot_l(_as01(tri), g)
    gt = _xdot_l(_as01(same), g)
    half = 0.5 * gt
    qg = q * jnp.exp(gc - half)
    kg = (k * jnp.exp(half - gc)).astype(BF16)
    kd = k * jnp.exp(gt - gc)
    qs = q * jnp.exp(gc)
    vb = v.astype(BF16)

    head = _div(_iota((1, HW), 1), HD)
    o = jnp.zeros((BLK, HW), F32)
    for h in range(HEADS):
        hm = head == h
        att = _bdot_nt(jnp.where(hm, qg, 0.0), kg)
        att = jnp.where(tri, att, 0.0)
        o = o + jnp.where(hm, jnp.dot(att.astype(BF16), vb, preferred_element_type=F32), 0.0)

    diag = _div(_iota((HW, HW), 0), HD) == _div(_iota((HW, HW), 1), HD)
    outs = []
    for j in range(BLK // CHUNK):
        sl = slice(j * CHUNK, (j + 1) * CHUNK)
        st = s_sc[...]
        outs.append(_bdot_nt(qs[sl], st))
        dec = jnp.exp(gt[j * CHUNK:j * CHUNK + 1, :])
        s_sc[...] = st * dec + jnp.where(diag, _bdot_tn(v[sl], kd[sl]), 0.0)
    o = o + jnp.concatenate(outs, axis=0)

    ss = _xdot_r(o * o, e_ref[...], 2)
    y_ref[0] = o * lax.rsqrt(ss * (1.0 / HD) + EPS) * gn_ref[...] * _silu(ag)

    @pl.when(tb == pl.num_programs(1) - 1)
    def _():
        st_ref[0] = s_sc[...]


def _hgrn_prompt(lb_p, gn, e256, z3, layer, l_true):
    b, t, _ = z3.shape
    kern = functools.partial(_hgrn_kernel, layer=layer, l_true=l_true)
    const = lambda shape: pl.BlockSpec(shape, lambda i, j: (0,) * len(shape))
    return pl.pallas_call(
        kern,
        grid=(b, t // BLK),
        in_specs=[const(lb_p.shape), const((1, HW)), const((HW, HW)),
                  pl.BlockSpec((1, BLK, 4 * HW), lambda i, j: (i, j, OFF_A // (4 * HW)))],
        out_specs=[pl.BlockSpec((1, BLK, HW), lambda i, j: (i, j, 0)),
                   pl.BlockSpec((1, HW, HW), lambda i, j: (i, 0, 0))],
        out_shape=[jax.ShapeDtypeStruct((b, t, HW), F32),
                   jax.ShapeDtypeStruct((b, HW, HW), F32)],
        scratch_shapes=[pltpu.VMEM((HW, HW), F32)],
        compiler_params=_params("parallel", "arbitrary"),
        name="hgrn_prompt",
    )(lb_p, gn, e256, z3)


def _delta_kernel(cw_ref, alog_ref, dtb_ref, gn_ref, e_ref, zc_ref, zg_ref, zs_ref,
                  y_ref, st_ref, s_sc, xe_sc, *, l_true):
    tb = pl.program_id(1)

    @pl.when(tb == 0)
    def _():
        s_sc[...] = jnp.zeros_like(s_sc)
        xe_sc[0:8, :] = jnp.zeros((8, C_CONV), F32)

    x = zc_ref[0]
    xe_sc[8:8 + BLK, :] = x
    cw = cw_ref[...]
    conv = jnp.zeros((BLK, C_CONV), F32)
    for w in range(CONV_W):
        start = 8 - (CONV_W - 1) + w
        conv = conv + xe_sc[start:start + BLK, :] * cw[w:w + 1, :]
    xe_sc[0:8, :] = x[BLK - 8:BLK, :]
    act = _silu(conv)
    qc, kc, v = act[:, 0:HW], act[:, HW:2 * HW], act[:, 2 * HW:3 * HW]
    e = e_ref[...]
    q = qc * lax.rsqrt(_xdot_r(qc * qc, e, 2) + EPS) * (HD ** -0.5)
    k = kc * lax.rsqrt(_xdot_r(kc * kc, e, 2) + EPS)

    zs = zs_ref[0]
    valid = (tb * BLK + _iota((BLK, 1), 0)) < l_true
    beta_s = jnp.where(valid, _sigmoid(zs), 0.0)
    la_s = jnp.where(valid, -jnp.exp(alog_ref[...]) * _softplus(zs + dtb_ref[...]), 0.0)
    src = _iota((128, HW), 0)
    dst_head = _div(_iota((128, HW), 1), HD)
    beta = _xdot_r(beta_s, _as01(src == dst_head))
    la = _xdot_r(la_s, _as01(src == dst_head + HEADS))

    same, tri, strict = _chunk_masks(BLK)
    gc = _xdot_l(_as01(tri), la)
    gt = _xdot_l(_as01(same), la)
    sel = _as01((_iota((8, HW), 1) == _iota((8, HW), 0) * HD) & (_iota((8, HW), 0) < HEADS))
    gc_t = _xdot_nt(sel, gc)
    eg = jnp.exp(gc)
    kd = k * jnp.exp(gt - gc)
    qg = q * eg
    rhs = jnp.concatenate([beta * eg * k, beta * v], axis=1)
    kb = k.astype(BF16)

    head = _div(_iota((1, HW), 1), HD)
    eye = _iota((BLK, BLK), 0) == _iota((BLK, BLK), 1)
    w_all = jnp.zeros((BLK, HW), F32)
    u_all = jnp.zeros((BLK, HW), F32)
    qks = []
    for h in range(HEADS):
        hm = head == h
        d = gc[:, h * HD:h * HD + 1] - gc_t[h:h + 1, :]
        rel = jnp.where(tri, jnp.exp(jnp.where(tri, d, 0.0)), 0.0)
        kk = _bdot_nt(jnp.where(hm, k, 0.0), kb)
        a = jnp.where(strict, beta[:, h * HD:h * HD + 1] * rel * kk, 0.0)
        inv = jnp.where(eye, 1.0, 0.0) - a
        p = a
        for _ in range(int(math.log2(CHUNK)) - 1):
            p = _dot3(p, p)
            inv = inv + _dot3(inv, p)
        sol = _dot3(inv, rhs)
        w_all = w_all + jnp.where(hm, sol[:, 0:HW], 0.0)
        u_all = u_all + jnp.where(hm, sol[:, HW:2 * HW], 0.0)
        qks.append(rel * _bdot_nt(jnp.where(hm, q, 0.0), kb))

    diag = _div(_iota((HW, HW), 0), HD) == _div(_iota((HW, HW), 1), HD)
    outs, us = [], []
    for j in range(BLK // CHUNK):
        sl = slice(j * CHUNK, (j + 1) * CHUNK)
        st = s_sc[...]
        sb = st.astype(BF16)
        u = u_all[sl] - jnp.dot(w_all[sl].astype(BF16), sb, preferred_element_type=F32)
        outs.append(jnp.dot(qg[sl].astype(BF16), sb, preferred_element_type=F32))
        us.append(u)
        dec = jnp.exp(gt[j * CHUNK:j * CHUNK + 1, :])
        s_sc[...] = st * dec + jnp.where(diag, _bdot_tn(kd[sl], u), 0.0)
    o = jnp.concatenate(outs, axis=0)
    ub = jnp.concatenate(us, axis=0).astype(BF16)
    for h in range(HEADS):
        o = o + jnp.where(head == h, jnp.dot(qks[h].astype(BF16), ub, preferred_element_type=F32), 0.0)

    ss = _xdot_r(o * o, e, 2)
    y_ref[0] = o * lax.rsqrt(ss * (1.0 / HD) + EPS) * gn_ref[...] * _silu(zg_ref[0])

    @pl.when(tb == pl.num_programs(1) - 1)
    def _():
        st_ref[0] = s_sc[...]


def _delta_prompt(cw, alog_p, dtb_p, gn, e256, z3, l_true):
    b, t, _ = z3.shape
    kern = functools.partial(_delta_kernel, l_true=l_true)
    const = lambda shape: pl.BlockSpec(shape, lambda i, j: (0,) * len(shape))
    return pl.pallas_call(
        kern,
        grid=(b, t // BLK),
        in_specs=[const((CONV_W, C_CONV)), const((1, 128)), const((1, 128)), const((1, HW)),
                  const((HW, HW)),
                  pl.BlockSpec((1, BLK, C_CONV), lambda i, j: (i, j, OFF_CQKV // C_CONV)),
                  pl.BlockSpec((1, BLK, HW), lambda i, j: (i, j, OFF_CG // HW)),
                  pl.BlockSpec((1, BLK, 128), lambda i, j: (i, j, OFF_CS // 128))],
        out_specs=[pl.BlockSpec((1, BLK, HW), lambda i, j: (i, j, 0)),
                   pl.BlockSpec((1, HW, HW), lambda i, j: (i, 0, 0))],
        out_shape=[jax.ShapeDtypeStruct((b, t, HW), F32),
                   jax.ShapeDtypeStruct((b, HW, HW), F32)],
        scratch_shapes=[pltpu.VMEM((HW, HW), F32), pltpu.VMEM((8 + BLK, C_CONV), F32)],
        compiler_params=_params("parallel", "arbitrary"),
        name="delta_prompt",
    )(cw, alog_p, dtb_p, gn, e256, z3, z3, z3)


def _sample_rec_kernel(lb_ref, gna_ref, gnc_ref, cw_ref, alog_ref, dtb_ref, z_ref, hst_ref, dst_ref,
                       cst_ref, ya_ref, yc_ref, hso_ref, dso_ref, *, layer):
    z = z_ref[0]
    eye = _iota((HW, HW), 0) == _iota((HW, HW), 1)

    def col(r):
        return jnp.sum(jnp.where(eye, r, 0.0), axis=1, keepdims=True)

    def head_norm(o, g, gate):
        return o * lax.rsqrt(jnp.mean(o * o, axis=-1, keepdims=True) + EPS) * g * _silu(gate)

    za = z[:, OFF_A:OFF_A + 4 * HW]
    aq, af = za[:, 0:HW], za[:, HW:2 * HW]
    v, ag = za[:, 2 * HW:3 * HW], za[:, 3 * HW:4 * HW]
    lb = _lower_bound(lb_ref[...], layer)
    f = lb + (1.0 - lb) * _sigmoid(af)
    fcol, kcol, qcol = col(f), col(1.0 - f), col(_silu(aq))
    for h in range(HEADS):
        sl = slice(h * HD, (h + 1) * HD)
        s = fcol[sl] * hst_ref[0, h] + kcol[sl] * v[:, sl]
        hso_ref[0, h] = s
        o = jnp.sum(qcol[sl] * s, axis=0, keepdims=True)
        ya_ref[0, :, sl] = head_norm(o, gna_ref[...], ag[:, sl])

    cst = cst_ref[0]
    cw = cw_ref[...]
    conv = z[:, OFF_CQKV:OFF_CQKV + C_CONV] * cw[CONV_W - 1:CONV_W]
    for w in range(CONV_W - 1):
        conv = conv + cst[w:w + 1] * cw[w:w + 1]
    act = _silu(conv)
    qc, kc, vc = act[:, 0:HW], act[:, HW:2 * HW], act[:, 2 * HW:3 * HW]
    cg = z[:, OFF_CG:OFF_CG + HW]
    zs = z[:, OFF_CS:OFF_CS + 128]
    beta_all = _sigmoid(zs)
    la_all = -jnp.exp(alog_ref[...]) * _softplus(zs + dtb_ref[...])
    qcol, kcol = col(qc), col(kc)
    for h in range(HEADS):
        sl = slice(h * HD, (h + 1) * HD)
        rq = lax.rsqrt(jnp.sum(qc[:, sl] * qc[:, sl], axis=-1, keepdims=True) + EPS) * (HD ** -0.5)
        rk = lax.rsqrt(jnp.sum(kc[:, sl] * kc[:, sl], axis=-1, keepdims=True) + EPS)
        kh, qh = kcol[sl] * rk, qcol[sl] * rq
        s = dst_ref[0, h] * jnp.exp(la_all[:, HEADS + h:HEADS + h + 1])
        u = beta_all[:, h:h + 1] * (vc[:, sl] - jnp.sum(kh * s, axis=0, keepdims=True))
        s = s + kh * u
        dso_ref[0, h] = s
        o = jnp.sum(qh * s, axis=0, keepdims=True)
        yc_ref[0, :, sl] = head_norm(o, gnc_ref[...], cg[:, sl])


def _sample_rec(lb_p, gna, gnc, cw, alog_p, dtb_p, zs3, hst, dst, cst, layer):
    nb = zs3.shape[0]
    kern = functools.partial(_sample_rec_kernel, layer=layer)
    const = lambda shape: pl.BlockSpec(shape, lambda i: (0,) * len(shape))
    st = pl.BlockSpec((1, HEADS, HD, HD), lambda i: (i, 0, 0, 0))
    yo = pl.BlockSpec((1, 1, HW), lambda i: (i, 0, 0))
    return pl.pallas_call(
        kern,
        grid=(nb,),
        in_specs=[const(lb_p.shape), const((1, HD)), const((1, HD)), const((CONV_W, C_CONV)),
                  const((1, 128)), const((1, 128)),
                  pl.BlockSpec((1, 1, NZ), lambda i: (i, 0, 0)), st, st,
                  pl.BlockSpec((1, CONV_W - 1, C_CONV), lambda i: (i, 0, 0))],
        out_specs=[yo, yo, st, st],
        out_shape=[jax.ShapeDtypeStruct((nb, 1, HW), F32)] * 2
                  + [jax.ShapeDtypeStruct((nb, HEADS, HD, HD), F32)] * 2,
        compiler_params=_params("parallel"),
        name="sample_rec",
    )(lb_p, gna, gnc, cw, alog_p, dtb_p, zs3, hst, dst, cst)


def _paged_kernel(pt_ref, lam_ref, gb_ref, q_ref, kn_ref, vn_ref, *rest, pps, lam_init, b_scale):
    k_refs, v_refs = rest[:pps], rest[pps:2 * pps]
    o_ref, m_sc, l_sc, acc_sc = rest[2 * pps:]
    s_id = pl.program_id(1)

    @pl.when(s_id == 0)
    def _():
        m_sc[...] = jnp.full_like(m_sc, -jnp.inf)
        l_sc[...] = jnp.zeros_like(l_sc)
        acc_sc[...] = jnp.zeros_like(acc_sc)

    rid = _iota((8, 128), 0)
    rhead = _div(rid, 2)
    comp_ok = _div(_iota((8, 128), 1), HD) == (rid & 1)

    def rows_by_head(r):
        out = jnp.zeros((8, 128), F32)
        for h in range(HEADS):
            out = jnp.where(rhead == h, r[:, h * 128:(h + 1) * 128], out)
        return out

    q8 = jnp.where(comp_ok, rows_by_head(q_ref[0]), 0.0)
    q8b = q8.astype(BF16)

    scores = []
    for i in range(pps):
        s = jnp.zeros((8, 128), F32)
        for h in range(HEADS):
            kh = k_refs[i][:, h, :].astype(BF16)
            sh = lax.dot_general(q8b, kh, (((1,), (1,)), ((), ())), preferred_element_type=F32)
            s = jnp.where(rhead == h, sh, s)
        scores.append(s)
    s_all = jnp.concatenate(scores, axis=1)
    m_old = m_sc[...]
    m_new = jnp.maximum(m_old, jnp.max(s_all, axis=1, keepdims=True))
    alpha = jnp.exp(m_old - m_new)
    p = jnp.exp(s_all - m_new)
    l_sc[...] = alpha * l_sc[...] + jnp.sum(p, axis=1, keepdims=True)
    pv = jnp.zeros((8, B_DV), F32)
    for i in range(pps):
        pb = p[:, i * 128:(i + 1) * 128].astype(BF16)
        for h in range(HEADS):
            vh = v_refs[i][:, h, :].astype(BF16)
            pv = pv + jnp.where(rhead == h, jnp.dot(pb, vh, preferred_element_type=F32), 0.0)
    acc_sc[...] = alpha * acc_sc[...] + pv
    m_sc[...] = m_new

    @pl.when(s_id == pl.num_programs(1) - 1)
    def _():
        s_new = jnp.sum(q8 * rows_by_head(kn_ref[0]), axis=1, keepdims=True)
        m_old = m_sc[...]
        m_fin = jnp.maximum(m_old, s_new)
        alpha = jnp.exp(m_old - m_fin)
        p_new = jnp.exp(s_new - m_fin)
        l_fin = alpha * l_sc[...] + p_new
        out = (alpha * acc_sc[...] + p_new * rows_by_head(vn_ref[0])) / l_fin
        lam = _lambda(lam_ref[...], lam_init)
        for h in range(HEADS):
            o = out[2 * h:2 * h + 1] - lam * out[2 * h + 1:2 * h + 2]
            y = o * lax.rsqrt(jnp.mean(o * o, axis=-1, keepdims=True) + EPS) * gb_ref[...]
            o_ref[0, :, h * B_DV:(h + 1) * B_DV] = y * b_scale


def _paged_attn(page_table, lam_p, gb, qs3, kn3, vn3, cache_k, cache_v, layer, lam_init, b_scale, pps):
    nb, n_pages = page_table.shape
    page, nh, dd = cache_k.shape[2:]
    kern = functools.partial(_paged_kernel, pps=pps, lam_init=lam_init, b_scale=b_scale)
    row = pl.BlockSpec((1, 1, 2 * HW), lambda b, s, pt: (b, 0, 0))

    def page_spec(i):
        return pl.BlockSpec((None, None, page, nh, dd),
                            lambda b, s, pt: (layer, pt[b, s * pps + i], 0, 0, 0))

    grid_spec = pltpu.PrefetchScalarGridSpec(
        num_scalar_prefetch=1,
        grid=(nb, n_pages // pps),
        in_specs=[pl.BlockSpec(lam_p.shape, lambda b, s, pt: (0, 0)),
                  pl.BlockSpec((1, B_DV), lambda b, s, pt: (0, 0)),
                  row, row, row]
                 + [page_spec(i) for i in range(pps)] * 2,
        out_specs=row,
        scratch_shapes=[pltpu.VMEM((8, 1), F32), pltpu.VMEM((8, 1), F32), pltpu.VMEM((8, B_DV), F32)],
    )
    return pl.pallas_call(
        kern,
        grid_spec=grid_spec,
        out_shape=jax.ShapeDtypeStruct((nb, 1, 2 * HW), F32),
        compiler_params=_params("parallel", "arbitrary"),
        name="paged_attn",
    )(page_table, lam_p, gb, qs3, kn3, vn3, *([cache_k] * pps), *([cache_v] * pps))


def _diag_blocks(s, transpose):
    blocks = jnp.stack([s[:, h * HD:(h + 1) * HD, h * HD:(h + 1) * HD] for h in range(HEADS)], axis=1)
    return jnp.swapaxes(blocks, -1, -2) if transpose else blocks


def kernel(x_prompt, x_sample, cache_k, cache_v, page_table, state_hgrn, state_delta, state_conv,
           meta_tokens, norm_mix, norm_ffn, w_in, w_out, hgrn_lower_bound, hgrn_out_norm,
           diff_q_norm, diff_k_norm, diff_lambda, diff_out_norm, conv_w, delta_a_log,
           delta_dt_bias, delta_out_norm, w_up, w_down):
    nb_p, seq, d_model = x_prompt.shape
    nb_s = x_sample.shape[0]
    depth = w_in.shape[0]
    l_true = N_META + seq
    l_pad = -(-l_true // BLK) * BLK
    n_past = page_table.shape[1] * cache_k.shape[2]

    xp = jnp.concatenate([jnp.broadcast_to(meta_tokens[None].astype(F32), (nb_p, N_META, d_model)),
                          x_prompt, jnp.zeros((nb_p, l_pad - l_true, d_model), F32)], axis=1)
    xp = xp.reshape(nb_p * l_pad, d_model)
    xs = x_sample.reshape(nb_s, d_model)

    e256 = jnp.kron(jnp.eye(HEADS, dtype=F32), jnp.ones((HD, HD), F32)).astype(BF16)
    e512 = jnp.kron(jnp.eye(2 * HEADS, dtype=F32), jnp.ones((HD, HD), F32)).astype(BF16)
    tabs_p = _rope_tables(jnp.arange(l_pad))
    tabs_s = _rope_tables(jnp.full((nb_s,), n_past))
    w_in_p = jnp.concatenate([w_in[:, :, 2560:3584], w_in[:, :, 0:2560], w_in[:, :, 3584:D_IN],
                              jnp.zeros(w_in.shape[:2] + (NZ - D_IN,), w_in.dtype)], axis=2).astype(BF16)
    w_out_b, w_up_b, w_down_b = w_out.astype(BF16), w_up.astype(BF16), w_down.astype(BF16)
    lane_pad = lambda a: jnp.pad(a[None, :], ((0, 0), (HEADS, 128 - 2 * HEADS)))

    outs = {k: [] for k in ("kp", "vp", "ks", "vs", "hp", "hs", "dp", "ds", "cp", "cs")}
    for l in range(depth):
        lam_init = 0.8 - 0.6 * math.exp(-0.3 * l)
        b_scale = 1.0 - lam_init
        g_mix, g_ffn = norm_mix[l][None], norm_ffn[l][None]
        gq, gk = jnp.tile(diff_q_norm[l], 2 * HEADS)[None], jnp.tile(diff_k_norm[l], 2 * HEADS)[None]
        gna, gnc = hgrn_out_norm[l][None], delta_out_norm[l][None]
        gna_t, gnc_t = jnp.tile(gna, (1, HEADS)), jnp.tile(gnc, (1, HEADS))
        gb = diff_out_norm[l][None]
        alog_p, dtb_p = lane_pad(delta_a_log[l]), lane_pad(delta_dt_bias[l])

        z3 = _proj(xp, g_mix, w_in_p[l], 256).reshape(nb_p, l_pad, NZ)
        ya, st_a = _hgrn_prompt(hgrn_lower_bound, gna_t, e256, z3, l, l_true)
        qn, kn = _qkrope(z3, gq, gk, e512, tabs_p, BLK)
        yb = _attn_prompt(diff_lambda[l], gb, qn, kn, z3, lam_init, b_scale)
        yc, st_c = _delta_prompt(conv_w[l], alog_p, dtb_p, gnc_t, e256, z3, l_true)
        rows = nb_p * l_pad
        xp = _merge_ffn(xp, ya.reshape(rows, -1), yb.reshape(rows, -1), yc.reshape(rows, -1),
                        w_out_b[l], g_ffn, w_up_b[l], w_down_b[l], 512, 1024)
        outs["kp"].append(kn[:, :l_true].reshape(nb_p, l_true, HEADS, 2 * HD))
        outs["vp"].append(z3[:, :l_true, OFF_BV:OFF_BV + HEADS * B_DV].reshape(nb_p, l_true, HEADS, B_DV))
        outs["hp"].append(_diag_blocks(st_a, True))
        outs["dp"].append(_diag_blocks(st_c, False))
        outs["cp"].append(z3[:, l_true - (CONV_W - 1):l_true, OFF_CQKV:OFF_CQKV + C_CONV])

        zs = _proj(xs, g_mix, w_in_p[l], nb_s)
        zs3 = zs.reshape(nb_s, 1, NZ)
        ya_s, yc_s, hs_new, ds_new = _sample_rec(hgrn_lower_bound, gna, gnc, conv_w[l], alog_p, dtb_p,
                                                 zs3, state_hgrn[l], state_delta[l], state_conv[l], l)
        qn_s, kn_s = _qkrope(zs.reshape(1, nb_s, NZ), gq, gk, e512, tabs_s, nb_s)
        vn_s = zs[:, OFF_BV:OFF_BV + HEADS * B_DV]
        yb_s = _paged_attn(page_table, diff_lambda[l], gb, qn_s.reshape(nb_s, 1, -1),
                           kn_s.reshape(nb_s, 1, -1), vn_s.reshape(nb_s, 1, -1),
                           cache_k, cache_v, l, lam_init, b_scale, 8)
        xs = _merge_ffn(xs, ya_s.reshape(nb_s, -1), yb_s.reshape(nb_s, -1), yc_s.reshape(nb_s, -1),
                        w_out_b[l], g_ffn, w_up_b[l], w_down_b[l], nb_s, 1024)
        outs["ks"].append(kn_s.reshape(nb_s, 1, HEADS, 2 * HD))
        outs["vs"].append(vn_s.reshape(nb_s, 1, HEADS, B_DV))
        outs["hs"].append(hs_new)
        outs["ds"].append(ds_new)
        outs["cs"].append(jnp.concatenate([state_conv[l][:, 1:], zs[:, None, OFF_CQKV:OFF_CQKV + C_CONV]],
                                          axis=1))

    y_prompt = xp.reshape(nb_p, l_pad, d_model)[:, N_META:l_true]
    y_sample = xs.reshape(nb_s, 1, d_model)
    return (y_prompt, y_sample) + tuple(jnp.stack(outs[k]) for k in
                                        ("kp", "vp", "ks", "vs", "hp", "hs", "dp", "ds", "cp", "cs"))
```

```python
import functools
import math

import jax
import jax.numpy as jnp
from jax import lax
from jax.experimental import pallas as pl
from jax.experimental.pallas import tpu as pltpu

F32 = jnp.float32
BF16 = jnp.bfloat16

EPS = 1e-6
N_META = 16
CHUNK = 16
BLK = 128
ABLK = 384
NROW = 2
HEADS = 4
HD = 64
HW = HEADS * HD
B_DV = 128
ROT_DIM = 16
ROPE_THETA = 500000.0
CONV_W = 4
C_CONV = 3 * HW
D_IN = 3592
NZ = 3712
OFF_CQKV, OFF_CG = 0, 768
OFF_A = 1024
OFF_BQ, OFF_BK, OFF_BV = 2048, 2560, 3072
OFF_CS = 3584

VMEM_LIMIT = 48 * 1024 * 1024


def _bdot(a, b):
    return jnp.dot(a.astype(BF16), b.astype(BF16), preferred_element_type=F32)


def _bdot_nt(a, b):
    return lax.dot_general(a.astype(BF16), b.astype(BF16), (((1,), (1,)), ((), ())),
                           preferred_element_type=F32)


def _bdot_tn(a, b):
    return lax.dot_general(a.astype(BF16), b.astype(BF16), (((0,), (0,)), ((), ())),
                           preferred_element_type=F32)


def _split(a, n):
    parts, r = [], a
    for i in range(n):
        p = r.astype(BF16)
        parts.append(p)
        if i + 1 < n:
            r = r - p.astype(F32)
    return parts


def _xdot_l(m01, a, n=3):
    return sum(jnp.dot(m01, p, preferred_element_type=F32) for p in _split(a, n))


def _xdot_r(a, m01, n=3):
    return sum(jnp.dot(p, m01, preferred_element_type=F32) for p in _split(a, n))


def _xdot_nt(m01, a, n=3):
    return sum(lax.dot_general(m01, p, (((1,), (1,)), ((), ())), preferred_element_type=F32)
               for p in _split(a, n))


def _dot3(a, b):
    ah, al = _split(a, 2)
    bh, bl = _split(b, 2)
    return (jnp.dot(ah, bh, preferred_element_type=F32)
            + jnp.dot(ah, bl, preferred_element_type=F32)
            + jnp.dot(al, bh, preferred_element_type=F32))


def _sigmoid(x):
    return 1.0 / (1.0 + jnp.exp(-x))


def _silu(x):
    return x * _sigmoid(x)


def _softplus(x):
    return jnp.maximum(x, 0.0) + jnp.log(1.0 + jnp.exp(-jnp.abs(x)))


def _iota(shape, dim):
    return lax.broadcasted_iota(jnp.int32, shape, dim)


def _div(x, n):
    return lax.shift_right_logical(x, int(math.log2(n)))


def _chunk_masks(n):
    r, c = _iota((n, n), 0), _iota((n, n), 1)
    same = _div(r, CHUNK) == _div(c, CHUNK)
    return same, same & (c <= r), same & (c < r)


def _as01(mask):
    return jnp.where(mask, 1.0, 0.0).astype(BF16)


def _lower_bound(p, layer):
    mx = jnp.max(p, axis=0, keepdims=True)
    ex = jnp.exp(p - mx)
    soft = ex / jnp.sum(ex, axis=0, keepdims=True)
    if layer == 0:
        return jnp.zeros_like(soft[0:1])
    return jnp.sum(soft[1:layer + 1], axis=0, keepdims=True)


def _lambda(lv, lam_init):
    a = jnp.sum(lv[0:1] * lv[1:2], axis=1, keepdims=True)
    b = jnp.sum(lv[2:3] * lv[3:4], axis=1, keepdims=True)
    return jnp.exp(a) - jnp.exp(b) + lam_init


def _row_tile(m, want):
    t = min(want, m)
    while m % t:
        t //= 2
    assert t == m or t % 8 == 0, (m, want)
    return t


def _params(*sem):
    return pltpu.CompilerParams(dimension_semantics=sem, vmem_limit_bytes=VMEM_LIMIT)


def _proj_kernel(x_ref, g_ref, w_ref, o_ref):
    x = x_ref[...]
    h = x * lax.rsqrt(jnp.mean(x * x, axis=-1, keepdims=True) + EPS) * g_ref[...]
    o_ref[...] = jnp.dot(h.astype(BF16), w_ref[...], preferred_element_type=F32)


def _proj(x, g, w, tm):
    m, d = x.shape
    n = w.shape[1]
    return pl.pallas_call(
        _proj_kernel,
        grid=(m // tm,),
        in_specs=[pl.BlockSpec((tm, d), lambda i: (i, 0)),
                  pl.BlockSpec((1, d), lambda i: (0, 0)),
                  pl.BlockSpec((d, n), lambda i: (0, 0))],
        out_specs=pl.BlockSpec((tm, n), lambda i: (i, 0)),
        out_shape=jax.ShapeDtypeStruct((m, n), F32),
        compiler_params=_params("parallel"),
        name="proj",
    )(x, g, w)


def _ffn_kernel(x_ref, ya_ref, yb_ref, yc_ref, wo_ref, g_ref, wu_ref, wd_ref, o_ref, h_sc, acc_sc):
    k = pl.program_id(1)

    @pl.when(k == 0)
    def _():
        mix = jnp.concatenate([ya_ref[...], yb_ref[...], yc_ref[...]], axis=-1)
        x1 = x_ref[...] + jnp.dot(mix.astype(BF16), wo_ref[...], preferred_element_type=F32)
        h = x1 * lax.rsqrt(jnp.mean(x1 * x1, axis=-1, keepdims=True) + EPS) * g_ref[...]
        h_sc[...] = h.astype(BF16)
        acc_sc[...] = x1

    hid = jnp.dot(h_sc[...], wu_ref[...], preferred_element_type=F32)
    hid = jnp.square(jnp.maximum(hid, 0.0))
    acc_sc[...] += jnp.dot(hid.astype(BF16), wd_ref[...], preferred_element_type=F32)

    @pl.when(k == pl.num_programs(1) - 1)
    def _():
        o_ref[...] = acc_sc[...]


def _merge_ffn(x, ya, yb, yc, wo, g, wu, wd, tm, tf):
    m, d = x.shape
    dff = wu.shape[1]
    row = lambda i, k: (i, 0)
    return pl.pallas_call(
        _ffn_kernel,
        grid=(m // tm, dff // tf),
        in_specs=[pl.BlockSpec((tm, d), row),
                  pl.BlockSpec((tm, ya.shape[1]), row),
                  pl.BlockSpec((tm, yb.shape[1]), row),
                  pl.BlockSpec((tm, yc.shape[1]), row),
                  pl.BlockSpec(wo.shape, lambda i, k: (0, 0)),
                  pl.BlockSpec((1, d), lambda i, k: (0, 0)),
                  pl.BlockSpec((d, tf), lambda i, k: (0, k)),
                  pl.BlockSpec((tf, d), lambda i, k: (k, 0))],
        out_specs=pl.BlockSpec((tm, d), row),
        out_shape=jax.ShapeDtypeStruct((m, d), F32),
        scratch_shapes=[pltpu.VMEM((tm, d), BF16), pltpu.VMEM((tm, d), F32)],
        compiler_params=_params("parallel", "arbitrary"),
        name="merge_ffn",
    )(x, ya, yb, yc, wo, g, wu, wd)


def _qkrope_kernel(gq_ref, gk_ref, e_ref, c_ref, s1_ref, s2_ref, q_ref, k_ref, qo_ref, ko_ref):
    e = e_ref[...]
    cos, s_next, s_prev = c_ref[...], s1_ref[...], s2_ref[...]

    def norm_rope(x, g, scale):
        ss = _xdot_r(x * x, e, 2)
        xn = x * lax.rsqrt(ss * (1.0 / HD) + EPS) * g
        outs = []
        for h in range(HEADS):
            s = xn[:, h * 128:(h + 1) * 128]
            r = (s * cos + pltpu.roll(s, 128 - ROT_DIM // 2, 1) * s_next
                 + pltpu.roll(s, ROT_DIM // 2, 1) * s_prev)
            outs.append(r * scale if scale != 1.0 else r)
        return jnp.concatenate(outs, axis=1)

    qo_ref[0] = norm_rope(q_ref[0], gq_ref[...], HD ** -0.5)
    ko_ref[0] = norm_rope(k_ref[0], gk_ref[...], 1.0)


def _qkrope(z3, gq, gk, e512, tabs, tt):
    b, t, _ = z3.shape
    w = 2 * HW
    const = lambda shape: pl.BlockSpec(shape, lambda i, j: (0,) * len(shape))
    tab = pl.BlockSpec((tt, 128), lambda i, j: (j, 0))
    out = pl.BlockSpec((1, tt, w), lambda i, j: (i, j, 0))
    return pl.pallas_call(
        _qkrope_kernel,
        grid=(b, t // tt),
        in_specs=[const((1, w)), const((1, w)), const((w, w)), tab, tab, tab,
                  pl.BlockSpec((1, tt, w), lambda i, j: (i, j, OFF_BQ // w)),
                  pl.BlockSpec((1, tt, w), lambda i, j: (i, j, OFF_BK // w))],
        out_specs=[out, out],
        out_shape=[jax.ShapeDtypeStruct((b, t, w), F32)] * 2,
        compiler_params=_params("parallel", "parallel"),
        name="qkrope",
    )(gq, gk, e512, *tabs, z3, z3)


def _rope_tables(pos):
    half = ROT_DIM // 2
    inv = jnp.power(ROPE_THETA, -jnp.arange(half, dtype=F32) * 2.0 / ROT_DIM)
    ang = pos.astype(F32)[:, None] * inv[None, :]
    cos, sin = jnp.cos(ang), jnp.sin(ang)
    n = pos.shape[0]
    one = jnp.ones((n, HD - ROT_DIM), F32)
    zero = jnp.zeros((n, HD - half), F32)
    c = jnp.concatenate([cos, cos, one], axis=1)
    s_next = jnp.concatenate([-sin, zero], axis=1)
    s_prev = jnp.concatenate([jnp.zeros((n, half), F32), sin, one * 0.0], axis=1)
    return tuple(jnp.tile(a, (1, 2)) for a in (c, s_next, s_prev))


def _attn_kernel(lam_ref, gb_ref, q_ref, k_ref, v_ref, o_ref, m_sc, a_sc, *, lam_init, b_scale):
    i = pl.program_id(2)
    q = q_ref[0]
    lane = _iota((1, 128), 1)
    qs = (jnp.where(lane < HD, q, 0.0).astype(BF16), jnp.where(lane >= HD, q, 0.0).astype(BF16))
    m_sc[...] = jnp.full_like(m_sc, -jnp.inf)
    a_sc[...] = jnp.zeros_like(a_sc)
    ones = jnp.ones((ABLK, 128), BF16)

    def block(j, keep):
        start = pl.multiple_of(j * ABLK, ABLK)
        kb = k_ref[0, pl.ds(start, ABLK), :].astype(BF16)
        vb = jnp.concatenate([v_ref[0, pl.ds(start, ABLK), :].astype(BF16), ones], axis=1)
        for c in range(2):
            s = lax.dot_general(qs[c], kb, (((1,), (1,)), ((), ())), preferred_element_type=F32)
            if keep is not None:
                s = jnp.where(keep, s, -jnp.inf)
            m_old = m_sc[c]
            m_new = jnp.maximum(m_old, jnp.max(s, axis=1, keepdims=True))
            alpha = jnp.exp(m_old - m_new)
            p = jnp.concatenate([jnp.exp(s[:, n * 128:(n + 1) * 128] - m_new) for n in range(ABLK // 128)],
                                axis=1)
            a_sc[c] = (jnp.concatenate([alpha, alpha], axis=1) * a_sc[c]
                       + jnp.dot(p.astype(BF16), vb, preferred_element_type=F32))
            m_sc[c] = m_new

    def body(j, carry):
        block(j, None)
        return carry

    lax.fori_loop(0, i, body, 0)
    block(i, _iota((ABLK, ABLK), 1) <= _iota((ABLK, ABLK), 0))

    lam = _lambda(lam_ref[...], lam_init)
    o = (a_sc[0, :, 0:B_DV] / a_sc[0, :, B_DV:2 * B_DV]
         - lam * (a_sc[1, :, 0:B_DV] / a_sc[1, :, B_DV:2 * B_DV]))
    y = o * lax.rsqrt(jnp.mean(o * o, axis=-1, keepdims=True) + EPS) * gb_ref[...]
    o_ref[0] = y * b_scale


def _attn_prompt(lam_p, gb, qn, kn, z3, lam_init, b_scale):
    b, t, _ = qn.shape
    kern = functools.partial(_attn_kernel, lam_init=lam_init, b_scale=b_scale)
    return pl.pallas_call(
        kern,
        grid=(b, HEADS, t // ABLK),
        in_specs=[pl.BlockSpec(lam_p.shape, lambda i, h, j: (0, 0)),
                  pl.BlockSpec((1, B_DV), lambda i, h, j: (0, 0)),
                  pl.BlockSpec((1, ABLK, 128), lambda i, h, j: (i, j, h)),
                  pl.BlockSpec((1, t, 128), lambda i, h, j: (i, 0, h)),
                  pl.BlockSpec((1, t, B_DV), lambda i, h, j: (i, 0, OFF_BV // B_DV + h))],
        out_specs=pl.BlockSpec((1, ABLK, B_DV), lambda i, h, j: (i, j, h)),
        out_shape=jax.ShapeDtypeStruct((b, t, HEADS * B_DV), F32),
        scratch_shapes=[pltpu.VMEM((2, ABLK, 128), F32), pltpu.VMEM((2, ABLK, 2 * B_DV), F32)],
        compiler_params=_params("parallel", "parallel", "arbitrary"),
        name="attn_prompt",
    )(lam_p, gb, qn, kn, z3)


def _hgrn_kernel(lb_ref, gn_ref, e_ref, z_ref, y_ref, st_ref, s_sc, *, layer, l_true):
    tb = pl.program_id(1)

    @pl.when(tb == 0)
    def _():
        s_sc[...] = jnp.zeros_like(s_sc)

    rows = range(z_ref.shape[0])
    pre = [_hgrn_prep(r, tb, lb_ref, z_ref, layer, l_true) for r in rows]

    diag = _div(_iota((HW, HW), 0), HD) == _div(_iota((HW, HW), 1), HD)
    outs = [[] for _ in rows]
    for j in range(BLK // CHUNK):
        sl = slice(j * CHUNK, (j + 1) * CHUNK)
        for r in rows:
            c = pre[r]
            st = s_sc[r]
            outs[r].append(_bdot_nt(c["qs"][sl], st))
            dec = jnp.exp(c["gt"][j * CHUNK:j * CHUNK + 1, :])
            s_sc[r] = st * dec + jnp.where(diag, _bdot_tn(c["v"][sl], c["kd"][sl]), 0.0)

    for r in rows:
        o = pre[r]["o"] + jnp.concatenate(outs[r], axis=0)
        ss = _xdot_r(o * o, e_ref[...], 2)
        y_ref[r] = o * lax.rsqrt(ss * (1.0 / HD) + EPS) * gn_ref[...] * _silu(pre[r]["ag"])

    @pl.when(tb == pl.num_programs(1) - 1)
    def _():
        st_ref[...] = s_sc[...]


def _hgrn_prep(r, tb, lb_ref, z_ref, layer, l_true):
    z = z_ref[r]
    aq, af = z[:, 0:HW], z[:, HW:2 * HW]
    v, ag = z[:, 2 * HW:3 * HW], z[:, 3 * HW:4 * HW]
    lb = _lower_bound(lb_ref[...], layer)
    f = lb + (1.0 - lb) * _sigmoid(af)
    valid = (tb * BLK + _iota((BLK, 1), 0)) < l_true
    k = jnp.where(valid, 1.0 - f, 0.0)
    g = jnp.where(valid, jnp.log(f), 0.0)
    q = _silu(aq)

    same, tri, _ = _chunk_masks(BLK)
    gc = _xdot_l(_as01(tri), g)
    gt = _xdot_l(_as01(same), g)
    half = 0.5 * gt
    qg = q * jnp.exp(gc - half)
    kg = (k * jnp.exp(half - gc)).astype(BF16)
    kd = k * jnp.exp(gt - gc)
    qs = q * jnp.exp(gc)
    vb = v.astype(BF16)

    head = _div(_iota((1, HW), 1), HD)
    o = jnp.zeros((BLK, HW), F32)
    for h in range(HEADS):
        hm = head == h
        att = _bdot_nt(jnp.where(hm, qg, 0.0), kg)
        att = jnp.where(tri, att, 0.0)
        o = o + jnp.where(hm, jnp.dot(att.astype(BF16), vb, preferred_element_type=F32), 0.0)
    return dict(o=o, qs=qs, v=v, kd=kd, gt=gt, ag=ag)


def _hgrn_prompt(lb_p, gn, e256, z3, layer, l_true):
    b, t, _ = z3.shape
    nr = NROW if b % NROW == 0 else 1
    kern = functools.partial(_hgrn_kernel, layer=layer, l_true=l_true)
    const = lambda shape: pl.BlockSpec(shape, lambda i, j: (0,) * len(shape))
    return pl.pallas_call(
        kern,
        grid=(b // nr, t // BLK),
        in_specs=[const(lb_p.shape), const((1, HW)), const((HW, HW)),
                  pl.BlockSpec((nr, BLK, 4 * HW), lambda i, j: (i, j, OFF_A // (4 * HW)))],
        out_specs=[pl.BlockSpec((nr, BLK, HW), lambda i, j: (i, j, 0)),
                   pl.BlockSpec((nr, HW, HW), lambda i, j: (i, 0, 0))],
        out_shape=[jax.ShapeDtypeStruct((b, t, HW), F32),
                   jax.ShapeDtypeStruct((b, HW, HW), F32)],
        scratch_shapes=[pltpu.VMEM((nr, HW, HW), F32)],
        compiler_params=_params("parallel", "arbitrary"),
        name="hgrn_prompt",
    )(lb_p, gn, e256, z3)


def _delta_kernel(cw_ref, alog_ref, dtb_ref, gn_ref, e_ref, zc_ref, zg_ref, zs_ref,
                  y_ref, st_ref, s_sc, xe_sc, *, l_true):
    tb = pl.program_id(1)

    @pl.when(tb == 0)
    def _():
        s_sc[...] = jnp.zeros_like(s_sc)
        xe_sc[:, 0:8, :] = jnp.zeros((xe_sc.shape[0], 8, C_CONV), F32)

    rows = range(zc_ref.shape[0])
    units = [(r, h) for r in rows for h in range(HEADS)]
    _, tri, strict = _chunk_masks(BLK)
    head = _div(_iota((1, HW), 1), HD)
    pre = [_delta_prep(r, tb, cw_ref, alog_ref, dtb_ref, e_ref, zc_ref, zs_ref, xe_sc, l_true) for r in rows]

    eye = jnp.where(_iota((BLK, BLK), 0) == _iota((BLK, BLK), 1), 1.0, 0.0)
    rel, inv, pw = {}, {}, {}
    for r, h in units:
        c = pre[r]
        d = c["gc"][:, h * HD:h * HD + 1] - c["gc_t"][h:h + 1, :]
        rel[r, h] = jnp.where(tri, jnp.exp(jnp.where(tri, d, 0.0)), 0.0)
        kk = _bdot_nt(jnp.where(head == h, c["k"], 0.0), c["kb"])
        pw[r, h] = jnp.where(strict, c["beta"][:, h * HD:h * HD + 1] * rel[r, h] * kk, 0.0)
        inv[r, h] = eye - pw[r, h]
    for _ in range(int(math.log2(CHUNK)) - 1):
        for u in units:
            pw[u] = _dot3(pw[u], pw[u])
        for u in units:
            inv[u] = inv[u] + _dot3(inv[u], pw[u])
    w_all = [jnp.zeros((BLK, HW), F32) for _ in rows]
    u_all = [jnp.zeros((BLK, HW), F32) for _ in rows]
    qks = {}
    for r, h in units:
        c = pre[r]
        sol = _dot3(inv[r, h], c["rhs"])
        w_all[r] = w_all[r] + jnp.where(head == h, sol[:, 0:HW], 0.0)
        u_all[r] = u_all[r] + jnp.where(head == h, sol[:, HW:2 * HW], 0.0)
        qks[r, h] = rel[r, h] * _bdot_nt(jnp.where(head == h, c["q"], 0.0), c["kb"])

    diag = _div(_iota((HW, HW), 0), HD) == _div(_iota((HW, HW), 1), HD)
    outs = [[] for _ in rows]
    us = [[] for _ in rows]
    for j in range(BLK // CHUNK):
        sl = slice(j * CHUNK, (j + 1) * CHUNK)
        for r in rows:
            c = pre[r]
            st = s_sc[r]
            sb = st.astype(BF16)
            u = u_all[r][sl] - jnp.dot(w_all[r][sl].astype(BF16), sb, preferred_element_type=F32)
            outs[r].append(jnp.dot(c["qg"][sl].astype(BF16), sb, preferred_element_type=F32))
            us[r].append(u)
            dec = jnp.exp(c["gt"][j * CHUNK:j * CHUNK + 1, :])
            s_sc[r] = st * dec + jnp.where(diag, _bdot_tn(c["kd"][sl], u), 0.0)

    for r in rows:
        o = jnp.concatenate(outs[r], axis=0)
        ub = jnp.concatenate(us[r], axis=0).astype(BF16)
        for h in range(HEADS):
            o = o + jnp.where(head == h, jnp.dot(qks[r, h].astype(BF16), ub, preferred_element_type=F32), 0.0)
        ss = _xdot_r(o * o, e_ref[...], 2)
        y_ref[r] = o * lax.rsqrt(ss * (1.0 / HD) + EPS) * gn_ref[...] * _silu(zg_ref[r])

    @pl.when(tb == pl.num_programs(1) - 1)
    def _():
        st_ref[...] = s_sc[...]


def _delta_prep(r, tb, cw_ref, alog_ref, dtb_ref, e_ref, zc_ref, zs_ref, xe_sc, l_true):
    x = zc_ref[r]
    xe_sc[r, 8:8 + BLK, :] = x
    cw = cw_ref[...]
    conv = jnp.zeros((BLK, C_CONV), F32)
    for w in range(CONV_W):
        start = 8 - (CONV_W - 1) + w
        conv = conv + xe_sc[r, start:start + BLK, :] * cw[w:w + 1, :]
    xe_sc[r, 0:8, :] = x[BLK - 8:BLK, :]
    act = _silu(conv)
    qc, kc, v = act[:, 0:HW], act[:, HW:2 * HW], act[:, 2 * HW:3 * HW]
    e = e_ref[...]
    q = qc * lax.rsqrt(_xdot_r(qc * qc, e, 2) + EPS) * (HD ** -0.5)
    k = kc * lax.rsqrt(_xdot_r(kc * kc, e, 2) + EPS)

    zs = zs_ref[r]
    valid = (tb * BLK + _iota((BLK, 1), 0)) < l_true
    beta_s = jnp.where(valid, _sigmoid(zs), 0.0)
    la_s = jnp.where(valid, -jnp.exp(alog_ref[...]) * _softplus(zs + dtb_ref[...]), 0.0)
    src = _iota((128, HW), 0)
    dst_head = _div(_iota((128, HW), 1), HD)
    beta = _xdot_r(beta_s, _as01(src == dst_head))
    la = _xdot_r(la_s, _as01(src == dst_head + HEADS))

    same, tri, strict = _chunk_masks(BLK)
    gc = _xdot_l(_as01(tri), la)
    gt = _xdot_l(_as01(same), la)
    sel = _as01((_iota((8, HW), 1) == _iota((8, HW), 0) * HD) & (_iota((8, HW), 0) < HEADS))
    gc_t = _xdot_nt(sel, gc)
    eg = jnp.exp(gc)
    return dict(gc=gc, gt=gt, gc_t=gc_t, k=k, kb=k.astype(BF16), q=q, beta=beta,
                kd=k * jnp.exp(gt - gc), qg=q * eg,
                rhs=jnp.concatenate([beta * eg * k, beta * v], axis=1))


def _delta_prompt(cw, alog_p, dtb_p, gn, e256, z3, l_true):
    b, t, _ = z3.shape
    nr = NROW if b % NROW == 0 else 1
    kern = functools.partial(_delta_kernel, l_true=l_true)
    const = lambda shape: pl.BlockSpec(shape, lambda i, j: (0,) * len(shape))
    return pl.pallas_call(
        kern,
        grid=(b // nr, t // BLK),
        in_specs=[const((CONV_W, C_CONV)), const((1, 128)), const((1, 128)), const((1, HW)),
                  const((HW, HW)),
                  pl.BlockSpec((nr, BLK, C_CONV), lambda i, j: (i, j, OFF_CQKV // C_CONV)),
                  pl.BlockSpec((nr, BLK, HW), lambda i, j: (i, j, OFF_CG // HW)),
                  pl.BlockSpec((nr, BLK, 128), lambda i, j: (i, j, OFF_CS // 128))],
        out_specs=[pl.BlockSpec((nr, BLK, HW), lambda i, j: (i, j, 0)),
                   pl.BlockSpec((nr, HW, HW), lambda i, j: (i, 0, 0))],
        out_shape=[jax.ShapeDtypeStruct((b, t, HW), F32),
                   jax.ShapeDtypeStruct((b, HW, HW), F32)],
        scratch_shapes=[pltpu.VMEM((nr, HW, HW), F32), pltpu.VMEM((nr, 8 + BLK, C_CONV), F32)],
        compiler_params=_params("parallel", "arbitrary"),
        name="delta_prompt",
    )(cw, alog_p, dtb_p, gn, e256, z3, z3, z3)


def _sample_rec_kernel(lb_ref, gna_ref, gnc_ref, cw_ref, alog_ref, dtb_ref, z_ref, hst_ref, dst_ref,
                       cst_ref, ya_ref, yc_ref, hso_ref, dso_ref, *, layer):
    z = z_ref[0]
    eye = _iota((HW, HW), 0) == _iota((HW, HW), 1)

    def col(r):
        return jnp.sum(jnp.where(eye, r, 0.0), axis=1, keepdims=True)

    def head_norm(o, g, gate):
        return o * lax.rsqrt(jnp.mean(o * o, axis=-1, keepdims=True) + EPS) * g * _silu(gate)

    za = z[:, OFF_A:OFF_A + 4 * HW]
    aq, af = za[:, 0:HW], za[:, HW:2 * HW]
    v, ag = za[:, 2 * HW:3 * HW], za[:, 3 * HW:4 * HW]
    lb = _lower_bound(lb_ref[...], layer)
    f = lb + (1.0 - lb) * _sigmoid(af)
    fcol, kcol, qcol = col(f), col(1.0 - f), col(_silu(aq))
    for h in range(HEADS):
        sl = slice(h * HD, (h + 1) * HD)
        s = fcol[sl] * hst_ref[0, h] + kcol[sl] * v[:, sl]
        hso_ref[0, h] = s
        o = jnp.sum(qcol[sl] * s, axis=0, keepdims=True)
        ya_ref[0, :, sl] = head_norm(o, gna_ref[...], ag[:, sl])

    cst = cst_ref[0]
    cw = cw_ref[...]
    conv = z[:, OFF_CQKV:OFF_CQKV + C_CONV] * cw[CONV_W - 1:CONV_W]
    for w in range(CONV_W - 1):
        conv = conv + cst[w:w + 1] * cw[w:w + 1]
    act = _silu(conv)
    qc, kc, vc = act[:, 0:HW], act[:, HW:2 * HW], act[:, 2 * HW:3 * HW]
    cg = z[:, OFF_CG:OFF_CG + HW]
    zs = z[:, OFF_CS:OFF_CS + 128]
    beta_all = _sigmoid(zs)
    la_all = -jnp.exp(alog_ref[...]) * _softplus(zs + dtb_ref[...])
    qcol, kcol = col(qc), col(kc)
    for h in range(HEADS):
        sl = slice(h * HD, (h + 1) * HD)
        rq = lax.rsqrt(jnp.sum(qc[:, sl] * qc[:, sl], axis=-1, keepdims=True) + EPS) * (HD ** -0.5)
        rk = lax.rsqrt(jnp.sum(kc[:, sl] * kc[:, sl], axis=-1, keepdims=True) + EPS)
        kh, qh = kcol[sl] * rk, qcol[sl] * rq
        s = dst_ref[0, h] * jnp.exp(la_all[:, HEADS + h:HEADS + h + 1])
        u = beta_all[:, h:h + 1] * (vc[:, sl] - jnp.sum(kh * s, axis=0, keepdims=True))
        s = s + kh * u
        dso_ref[0, h] = s
        o = jnp.sum(qh * s, axis=0, keepdims=True)
        yc_ref[0, :, sl] = head_norm(o, gnc_ref[...], cg[:, sl])


def _sample_rec(lb_p, gna, gnc, cw, alog_p, dtb_p, zs3, hst, dst, cst, layer):
    nb = zs3.shape[0]
    kern = functools.partial(_sample_rec_kernel, layer=layer)
    const = lambda shape: pl.BlockSpec(shape, lambda i: (0,) * len(shape))
    st = pl.BlockSpec((1, HEADS, HD, HD), lambda i: (i, 0, 0, 0))
    yo = pl.BlockSpec((1, 1, HW), lambda i: (i, 0, 0))
    return pl.pallas_call(
        kern,
        grid=(nb,),
        in_specs=[const(lb_p.shape), const((1, HD)), const((1, HD)), const((CONV_W, C_CONV)),
                  const((1, 128)), const((1, 128)),
                  pl.BlockSpec((1, 1, NZ), lambda i: (i, 0, 0)), st, st,
                  pl.BlockSpec((1, CONV_W - 1, C_CONV), lambda i: (i, 0, 0))],
        out_specs=[yo, yo, st, st],
        out_shape=[jax.ShapeDtypeStruct((nb, 1, HW), F32)] * 2
                  + [jax.ShapeDtypeStruct((nb, HEADS, HD, HD), F32)] * 2,
        compiler_params=_params("parallel"),
        name="sample_rec",
    )(lb_p, gna, gnc, cw, alog_p, dtb_p, zs3, hst, dst, cst)


def _paged_kernel(pt_ref, lam_ref, gb_ref, q_ref, kn_ref, vn_ref, *rest, pps, lam_init, b_scale):
    k_refs, v_refs = rest[:pps], rest[pps:2 * pps]
    o_ref, m_sc, l_sc, acc_sc = rest[2 * pps:]
    s_id = pl.program_id(1)
    rows = k_refs[0].shape[0]
    rhead = _iota((8, 128), 0) & (HEADS - 1)

    def by_head(r):
        out = jnp.zeros((8, 128), F32)
        for h in range(HEADS):
            out = jnp.where(rhead == h, r[:, h * 128:(h + 1) * 128], out)
        return out

    def merge(x):
        return pltpu.roll(x, HEADS, 0)

    @pl.when(s_id == 0)
    def _():
        m_sc[...] = jnp.full_like(m_sc, -jnp.inf)
        l_sc[...] = jnp.zeros_like(l_sc)
        acc_sc[...] = jnp.zeros_like(acc_sc)

    q8 = by_head(q_ref[0])
    comp_sum = _as01(_div(_iota((128, 256), 0), HD) == _div(_iota((128, 256), 1), 128))
    for i in range(pps):
        prod = (k_refs[i][...].reshape(rows // 8, 8, 128) * q8).reshape(rows, 128)
        s2 = jnp.dot(prod.astype(BF16), comp_sum, preferred_element_type=F32)
        v3 = v_refs[i][...].reshape(rows // 8, 8, B_DV)
        for c in range(2):
            s3 = s2[:, c * 128:(c + 1) * 128].reshape(rows // 8, 8, 128)
            mx = jnp.max(s3, axis=0)
            m_old = m_sc[c]
            m_new = jnp.maximum(m_old, jnp.maximum(mx, merge(mx)))
            alpha = jnp.exp(m_old - m_new)
            p = jnp.exp(s3 - m_new)
            l_sc[c] = alpha * l_sc[c] + jnp.sum(p, axis=0)
            acc_sc[c] = alpha * acc_sc[c] + jnp.sum(p * v3, axis=0)
            m_sc[c] = m_new

    @pl.when(s_id == pl.num_programs(1) - 1)
    def _():
        lam = _lambda(lam_ref[...], lam_init)
        qk_new = q8 * by_head(kn_ref[0])
        v_new = by_head(vn_ref[0])
        lane = _iota((1, 128), 1)
        outs = []
        for c in range(2):
            s_new = jnp.sum(jnp.where(_div(lane, HD) == c, qk_new, 0.0), axis=1, keepdims=True)
            m_old = m_sc[c]
            m_fin = jnp.maximum(m_old, s_new)
            alpha = jnp.exp(m_old - m_fin)
            p_new = jnp.exp(s_new - m_fin)
            l_fin = alpha * (l_sc[c] + merge(l_sc[c])) + p_new
            a_fin = alpha * (acc_sc[c] + merge(acc_sc[c])) + p_new * v_new
            outs.append(a_fin / l_fin)
        o = outs[0] - lam * outs[1]
        y = o * lax.rsqrt(jnp.mean(o * o, axis=-1, keepdims=True) + EPS) * gb_ref[...] * b_scale
        for h in range(HEADS):
            o_ref[0, :, h * B_DV:(h + 1) * B_DV] = y[h:h + 1]


def _paged_attn(page_table, lam_p, gb, qs3, kn3, vn3, cache_k, cache_v, layer, lam_init, b_scale, pps):
    nb, n_pages = page_table.shape
    depth, n_pool, page, nh, dd = cache_k.shape
    assert nh == HEADS and dd == 128 and cache_v.shape == cache_k.shape
    cache_k = cache_k.reshape(depth, n_pool, page * nh, dd)
    cache_v = cache_v.reshape(depth, n_pool, page * nh, dd)
    kern = functools.partial(_paged_kernel, pps=pps, lam_init=lam_init, b_scale=b_scale)
    row = pl.BlockSpec((1, 1, 2 * HW), lambda b, s, pt: (b, 0, 0))

    def page_spec(i):
        return pl.BlockSpec((None, None, page * nh, dd),
                            lambda b, s, pt: (layer, pt[b, s * pps + i], 0, 0))

    grid_spec = pltpu.PrefetchScalarGridSpec(
        num_scalar_prefetch=1,
        grid=(nb, n_pages // pps),
        in_specs=[pl.BlockSpec(lam_p.shape, lambda b, s, pt: (0, 0)),
                  pl.BlockSpec((1, B_DV), lambda b, s, pt: (0, 0)),
                  row, row, row]
                 + [page_spec(i) for i in range(pps)] * 2,
        out_specs=row,
        scratch_shapes=[pltpu.VMEM((2, 8, B_DV), F32)] * 3,
    )
    return pl.pallas_call(
        kern,
        grid_spec=grid_spec,
        out_shape=jax.ShapeDtypeStruct((nb, 1, 2 * HW), F32),
        compiler_params=_params("parallel", "arbitrary"),
        name="paged_attn",
    )(page_table, lam_p, gb, qs3, kn3, vn3, *([cache_k] * pps), *([cache_v] * pps))


def _diag_blocks(s, transpose):
    blocks = jnp.stack([s[:, h * HD:(h + 1) * HD, h * HD:(h + 1) * HD] for h in range(HEADS)], axis=1)
    return jnp.swapaxes(blocks, -1, -2) if transpose else blocks


def kernel(x_prompt, x_sample, cache_k, cache_v, page_table, state_hgrn, state_delta, state_conv,
           meta_tokens, norm_mix, norm_ffn, w_in, w_out, hgrn_lower_bound, hgrn_out_norm,
           diff_q_norm, diff_k_norm, diff_lambda, diff_out_norm, conv_w, delta_a_log,
           delta_dt_bias, delta_out_norm, w_up, w_down):
    nb_p, seq, d_model = x_prompt.shape
    nb_s = x_sample.shape[0]
    depth = w_in.shape[0]
    l_true = N_META + seq
    l_pad = -(-l_true // ABLK) * ABLK
    assert ABLK % BLK == 0 and x_sample.shape[1] == 1
    n_past = page_table.shape[1] * cache_k.shape[2]

    xp = jnp.concatenate([jnp.broadcast_to(meta_tokens[None].astype(F32), (nb_p, N_META, d_model)),
                          x_prompt, jnp.zeros((nb_p, l_pad - l_true, d_model), F32)], axis=1)
    xp = xp.reshape(nb_p * l_pad, d_model)
    xs = x_sample.reshape(nb_s, d_model)

    e256 = jnp.kron(jnp.eye(HEADS, dtype=F32), jnp.ones((HD, HD), F32)).astype(BF16)
    e512 = jnp.kron(jnp.eye(2 * HEADS, dtype=F32), jnp.ones((HD, HD), F32)).astype(BF16)
    tabs_p = _rope_tables(jnp.arange(l_pad))
    tabs_s = _rope_tables(jnp.full((nb_s,), n_past))
    w_in_p = jnp.concatenate([w_in[:, :, 2560:3584], w_in[:, :, 0:2560], w_in[:, :, 3584:D_IN],
                              jnp.zeros(w_in.shape[:2] + (NZ - D_IN,), w_in.dtype)], axis=2).astype(BF16)
    w_out_b, w_up_b, w_down_b = w_out.astype(BF16), w_up.astype(BF16), w_down.astype(BF16)
    lane_pad = lambda a: jnp.pad(a[None, :], ((0, 0), (HEADS, 128 - 2 * HEADS)))

    outs = {k: [] for k in ("kp", "vp", "ks", "vs", "hp", "hs", "dp", "ds", "cp", "cs")}
    for l in range(depth):
        lam_init = 0.8 - 0.6 * math.exp(-0.3 * l)
        b_scale = 1.0 - lam_init
        g_mix, g_ffn = norm_mix[l][None], norm_ffn[l][None]
        gq, gk = jnp.tile(diff_q_norm[l], 2 * HEADS)[None], jnp.tile(diff_k_norm[l], 2 * HEADS)[None]
        gna, gnc = hgrn_out_norm[l][None], delta_out_norm[l][None]
        gna_t, gnc_t = jnp.tile(gna, (1, HEADS)), jnp.tile(gnc, (1, HEADS))
        gb = diff_out_norm[l][None]
        alog_p, dtb_p = lane_pad(delta_a_log[l]), lane_pad(delta_dt_bias[l])

        z3 = _proj(xp, g_mix, w_in_p[l], _row_tile(xp.shape[0], 256)).reshape(nb_p, l_pad, NZ)
        ya, st_a = _hgrn_prompt(hgrn_lower_bound, gna_t, e256, z3, l, l_true)
        qn, kn = _qkrope(z3, gq, gk, e512, tabs_p, BLK)
        yb = _attn_prompt(diff_lambda[l], gb, qn, kn, z3, lam_init, b_scale)
        yc, st_c = _delta_prompt(conv_w[l], alog_p, dtb_p, gnc_t, e256, z3, l_true)
        rows = nb_p * l_pad
        xp = _merge_ffn(xp, ya.reshape(rows, -1), yb.reshape(rows, -1), yc.reshape(rows, -1),
                        w_out_b[l], g_ffn, w_up_b[l], w_down_b[l], _row_tile(rows, 512), 1024)
        outs["kp"].append(kn[:, :l_true].reshape(nb_p, l_true, HEADS, 2 * HD))
        outs["vp"].append(z3[:, :l_true, OFF_BV:OFF_BV + HEADS * B_DV].reshape(nb_p, l_true, HEADS, B_DV))
        outs["hp"].append(_diag_blocks(st_a, True))
        outs["dp"].append(_diag_blocks(st_c, False))
        outs["cp"].append(z3[:, l_true - (CONV_W - 1):l_true, OFF_CQKV:OFF_CQKV + C_CONV])

        zs = _proj(xs, g_mix, w_in_p[l], nb_s)
        zs3 = zs.reshape(nb_s, 1, NZ)
        ya_s, yc_s, hs_new, ds_new = _sample_rec(hgrn_lower_bound, gna, gnc, conv_w[l], alog_p, dtb_p,
                                                 zs3, state_hgrn[l], state_delta[l], state_conv[l], l)
        qn_s, kn_s = _qkrope(zs.reshape(1, nb_s, NZ), gq, gk, e512, tabs_s, nb_s)
        vn_s = zs[:, OFF_BV:OFF_BV + HEADS * B_DV]
        yb_s = _paged_attn(page_table, diff_lambda[l], gb, qn_s.reshape(nb_s, 1, -1),
                           kn_s.reshape(nb_s, 1, -1), vn_s.reshape(nb_s, 1, -1),
                           cache_k, cache_v, l, lam_init, b_scale, 8)
        xs = _merge_ffn(xs, ya_s.reshape(nb_s, -1), yb_s.reshape(nb_s, -1), yc_s.reshape(nb_s, -1),
                        w_out_b[l], g_ffn, w_up_b[l], w_down_b[l], nb_s, 1024)
        outs["ks"].append(kn_s.reshape(nb_s, 1, HEADS, 2 * HD))
        outs["vs"].append(vn_s.reshape(nb_s, 1, HEADS, B_DV))
        outs["hs"].append(hs_new)
        outs["ds"].append(ds_new)
        outs["cs"].append(jnp.concatenate([state_conv[l][:, 1:], zs[:, None, OFF_CQKV:OFF_CQKV + C_CONV]],
                                          axis=1))

    y_prompt = xp.reshape(nb_p, l_pad, d_model)[:, N_META:l_true]
    y_sample = xs.reshape(nb_s, 1, d_model)
    return (y_prompt, y_sample) + tuple(jnp.stack(outs[k]) for k in
                                        ("kp", "vp", "ks", "vs", "hp", "hs", "dp", "ds", "cp", "cs"))
```

```python
import functools
import math

import jax
import jax.numpy as jnp
from jax import lax
from jax.experimental import pallas as pl
from jax.experimental.pallas import tpu as pltpu

F32 = jnp.float32
BF16 = jnp.bfloat16

EPS = 1e-6
N_META = 16
CHUNK = 16
BLK = 128
ABLK = 384
NROW = 2
PAGES_PER_STEP = 16
HEADS = 4
HD = 64
HW = HEADS * HD
B_DV = 128
ROT_DIM = 16
ROPE_THETA = 500000.0
Q_SCALE = HD ** -0.5 * math.log2(math.e)
CONV_W = 4
C_CONV = 3 * HW
D_IN = 3592
NZ = 3712
OFF_CQKV, OFF_CG = 0, 768
OFF_A = 1024
OFF_BQ, OFF_BK, OFF_BV = 2048, 2560, 3072
OFF_CS = 3584

VMEM_LIMIT = 48 * 1024 * 1024


def _bdot(a, b):
    return jnp.dot(a.astype(BF16), b.astype(BF16), preferred_element_type=F32)


def _bdot_nt(a, b):
    return lax.dot_general(a.astype(BF16), b.astype(BF16), (((1,), (1,)), ((), ())),
                           preferred_element_type=F32)


def _bdot_tn(a, b):
    return lax.dot_general(a.astype(BF16), b.astype(BF16), (((0,), (0,)), ((), ())),
                           preferred_element_type=F32)


def _split(a, n):
    parts, r = [], a
    for i in range(n):
        p = r.astype(BF16)
        parts.append(p)
        if i + 1 < n:
            r = r - p.astype(F32)
    return parts


def _xdot_l(m01, a, n=3):
    return sum(jnp.dot(m01, p, preferred_element_type=F32) for p in _split(a, n))


def _xdot_r(a, m01, n=3):
    return sum(jnp.dot(p, m01, preferred_element_type=F32) for p in _split(a, n))


def _xdot_nt(m01, a, n=3):
    return sum(lax.dot_general(m01, p, (((1,), (1,)), ((), ())), preferred_element_type=F32)
               for p in _split(a, n))


def _bdot_rows(lhs, b):
    m = lhs[0].shape[0]
    r = jnp.dot(jnp.concatenate([l.astype(BF16) for l in lhs], axis=0), b.astype(BF16),
                preferred_element_type=F32)
    return [r[i * m:(i + 1) * m] for i in range(len(lhs))]


def _sigmoid(x):
    return 1.0 / (1.0 + jnp.exp(-x))


def _silu(x):
    return x * _sigmoid(x)


def _softplus(x):
    return jnp.maximum(x, 0.0) + jnp.log(1.0 + jnp.exp(-jnp.abs(x)))


def _iota(shape, dim):
    return lax.broadcasted_iota(jnp.int32, shape, dim)


def _div(x, n):
    return lax.shift_right_logical(x, int(math.log2(n)))


def _chunk_masks(n):
    r, c = _iota((n, n), 0), _iota((n, n), 1)
    same = _div(r, CHUNK) == _div(c, CHUNK)
    return same, same & (c <= r), same & (c < r)


def _as01(mask):
    return jnp.where(mask, 1.0, 0.0).astype(BF16)


def _lower_bound(p, layer):
    mx = jnp.max(p, axis=0, keepdims=True)
    ex = jnp.exp(p - mx)
    soft = ex / jnp.sum(ex, axis=0, keepdims=True)
    if layer == 0:
        return jnp.zeros_like(soft[0:1])
    return jnp.sum(soft[1:layer + 1], axis=0, keepdims=True)


def _lambda(lv, lam_init):
    a = jnp.sum(lv[0:1] * lv[1:2], axis=1, keepdims=True)
    b = jnp.sum(lv[2:3] * lv[3:4], axis=1, keepdims=True)
    return jnp.exp(a) - jnp.exp(b) + lam_init


def _row_tile(m, want):
    t = min(want, m)
    while m % t:
        t //= 2
    assert t == m or t % 8 == 0, (m, want)
    return t


def _params(*sem):
    return pltpu.CompilerParams(dimension_semantics=sem, vmem_limit_bytes=VMEM_LIMIT)


def _proj_kernel(x_ref, g_ref, w_ref, o_ref):
    x = x_ref[...]
    h = x * lax.rsqrt(jnp.mean(x * x, axis=-1, keepdims=True) + EPS) * g_ref[...]
    o_ref[...] = jnp.dot(h.astype(BF16), w_ref[...], preferred_element_type=F32)


def _proj(x, g, w, tm):
    m, d = x.shape
    n = w.shape[1]
    return pl.pallas_call(
        _proj_kernel,
        grid=(m // tm,),
        in_specs=[pl.BlockSpec((tm, d), lambda i: (i, 0)),
                  pl.BlockSpec((1, d), lambda i: (0, 0)),
                  pl.BlockSpec((d, n), lambda i: (0, 0))],
        out_specs=pl.BlockSpec((tm, n), lambda i: (i, 0)),
        out_shape=jax.ShapeDtypeStruct((m, n), F32),
        compiler_params=_params("parallel"),
        name="proj",
    )(x, g, w)


def _ffn_kernel(x_ref, ya_ref, yb_ref, yc_ref, wo_ref, g_ref, wu_ref, wd_ref, o_ref, h_sc, acc_sc):
    k = pl.program_id(1)

    @pl.when(k == 0)
    def _():
        mix = jnp.concatenate([ya_ref[...], yb_ref[...], yc_ref[...]], axis=-1)
        x1 = x_ref[...] + jnp.dot(mix.astype(BF16), wo_ref[...], preferred_element_type=F32)
        h = x1 * lax.rsqrt(jnp.mean(x1 * x1, axis=-1, keepdims=True) + EPS) * g_ref[...]
        h_sc[...] = h.astype(BF16)
        acc_sc[...] = x1

    hid = jnp.dot(h_sc[...], wu_ref[...], preferred_element_type=F32)
    hid = jnp.square(jnp.maximum(hid, 0.0))
    acc_sc[...] += jnp.dot(hid.astype(BF16), wd_ref[...], preferred_element_type=F32)

    @pl.when(k == pl.num_programs(1) - 1)
    def _():
        o_ref[...] = acc_sc[...]


def _merge_ffn(x, ya, yb, yc, wo, g, wu, wd, tm, tf):
    m, d = x.shape
    dff = wu.shape[1]
    row = lambda i, k: (i, 0)
    return pl.pallas_call(
        _ffn_kernel,
        grid=(m // tm, dff // tf),
        in_specs=[pl.BlockSpec((tm, d), row),
                  pl.BlockSpec((tm, ya.shape[1]), row),
                  pl.BlockSpec((tm, yb.shape[1]), row),
                  pl.BlockSpec((tm, yc.shape[1]), row),
                  pl.BlockSpec(wo.shape, lambda i, k: (0, 0)),
                  pl.BlockSpec((1, d), lambda i, k: (0, 0)),
                  pl.BlockSpec((d, tf), lambda i, k: (0, k)),
                  pl.BlockSpec((tf, d), lambda i, k: (k, 0))],
        out_specs=pl.BlockSpec((tm, d), row),
        out_shape=jax.ShapeDtypeStruct((m, d), F32),
        scratch_shapes=[pltpu.VMEM((tm, d), BF16), pltpu.VMEM((tm, d), F32)],
        compiler_params=_params("parallel", "arbitrary"),
        name="merge_ffn",
    )(x, ya, yb, yc, wo, g, wu, wd)


def _qkrope_kernel(gq_ref, gk_ref, e_ref, c_ref, s1_ref, s2_ref, q_ref, k_ref, qo_ref, ko_ref):
    e = e_ref[...]
    cos, s_next, s_prev = c_ref[...], s1_ref[...], s2_ref[...]

    def norm_rope(x, g, scale):
        ss = _xdot_r(x * x, e, 2)
        xn = x * lax.rsqrt(ss * (1.0 / HD) + EPS) * g
        outs = []
        for h in range(HEADS):
            s = xn[:, h * 128:(h + 1) * 128]
            r = (s * cos + pltpu.roll(s, 128 - ROT_DIM // 2, 1) * s_next
                 + pltpu.roll(s, ROT_DIM // 2, 1) * s_prev)
            outs.append(r * scale if scale != 1.0 else r)
        return jnp.concatenate(outs, axis=1)

    qo_ref[0] = norm_rope(q_ref[0], gq_ref[...], Q_SCALE)
    ko_ref[0] = norm_rope(k_ref[0], gk_ref[...], 1.0)


def _qkrope(z3, gq, gk, e512, tabs, tt):
    b, t, _ = z3.shape
    w = 2 * HW
    const = lambda shape: pl.BlockSpec(shape, lambda i, j: (0,) * len(shape))
    tab = pl.BlockSpec((tt, 128), lambda i, j: (j, 0))
    out = pl.BlockSpec((1, tt, w), lambda i, j: (i, j, 0))
    return pl.pallas_call(
        _qkrope_kernel,
        grid=(b, t // tt),
        in_specs=[const((1, w)), const((1, w)), const((w, w)), tab, tab, tab,
                  pl.BlockSpec((1, tt, w), lambda i, j: (i, j, OFF_BQ // w)),
                  pl.BlockSpec((1, tt, w), lambda i, j: (i, j, OFF_BK // w))],
        out_specs=[out, out],
        out_shape=[jax.ShapeDtypeStruct((b, t, w), F32)] * 2,
        compiler_params=_params("parallel", "parallel"),
        name="qkrope",
    )(gq, gk, e512, *tabs, z3, z3)


def _rope_tables(pos):
    half = ROT_DIM // 2
    inv = jnp.power(ROPE_THETA, -jnp.arange(half, dtype=F32) * 2.0 / ROT_DIM)
    ang = pos.astype(F32)[:, None] * inv[None, :]
    cos, sin = jnp.cos(ang), jnp.sin(ang)
    n = pos.shape[0]
    one = jnp.ones((n, HD - ROT_DIM), F32)
    zero = jnp.zeros((n, HD - half), F32)
    c = jnp.concatenate([cos, cos, one], axis=1)
    s_next = jnp.concatenate([-sin, zero], axis=1)
    s_prev = jnp.concatenate([jnp.zeros((n, half), F32), sin, one * 0.0], axis=1)
    return tuple(jnp.tile(a, (1, 2)) for a in (c, s_next, s_prev))


def _attn_kernel(lam_ref, gb_ref, q_ref, k_ref, v_ref, o_ref, m_sc, a_sc, *, lam_init, b_scale):
    i = pl.program_id(2)
    q = q_ref[0]
    lane = _iota((1, 128), 1)
    qs = (jnp.where(lane < HD, q, 0.0).astype(BF16), jnp.where(lane >= HD, q, 0.0).astype(BF16))
    m_sc[...] = jnp.full_like(m_sc, -jnp.inf)
    a_sc[...] = jnp.zeros_like(a_sc)
    ones = jnp.ones((ABLK, 128), BF16)

    def block(j, keep):
        start = pl.multiple_of(j * ABLK, ABLK)
        kb = k_ref[0, pl.ds(start, ABLK), :].astype(BF16)
        vb = jnp.concatenate([v_ref[0, pl.ds(start, ABLK), :].astype(BF16), ones], axis=1)
        scores = [lax.dot_general(qs[c], kb, (((1,), (1,)), ((), ())), preferred_element_type=F32)
                  for c in range(2)]
        for c in range(2):
            s = scores[c]
            if keep is not None:
                s = jnp.where(keep, s, -jnp.inf)
            m_old = m_sc[c]
            m_new = jnp.maximum(m_old, jnp.max(s, axis=1, keepdims=True))
            alpha = jnp.exp2(m_old - m_new)
            p = jnp.concatenate([jnp.exp2(s[:, n * 128:(n + 1) * 128] - m_new) for n in range(ABLK // 128)],
                                axis=1)
            a_sc[c] = (jnp.concatenate([alpha, alpha], axis=1) * a_sc[c]
                       + jnp.dot(p.astype(BF16), vb, preferred_element_type=F32))
            m_sc[c] = m_new

    def body(j, carry):
        block(j, None)
        return carry

    lax.fori_loop(0, i, body, 0)
    block(i, _iota((ABLK, ABLK), 1) <= _iota((ABLK, ABLK), 0))

    lam = _lambda(lam_ref[...], lam_init)
    o = (a_sc[0, :, 0:B_DV] / a_sc[0, :, B_DV:2 * B_DV]
         - lam * (a_sc[1, :, 0:B_DV] / a_sc[1, :, B_DV:2 * B_DV]))
    y = o * lax.rsqrt(jnp.mean(o * o, axis=-1, keepdims=True) + EPS) * gb_ref[...]
    o_ref[0] = y * b_scale


def _attn_prompt(lam_p, gb, qn, kn, z3, lam_init, b_scale):
    b, t, _ = qn.shape
    kern = functools.partial(_attn_kernel, lam_init=lam_init, b_scale=b_scale)
    return pl.pallas_call(
        kern,
        grid=(b, HEADS, t // ABLK),
        in_specs=[pl.BlockSpec(lam_p.shape, lambda i, h, j: (0, 0)),
                  pl.BlockSpec((1, B_DV), lambda i, h, j: (0, 0)),
                  pl.BlockSpec((1, ABLK, 128), lambda i, h, j: (i, j, h)),
                  pl.BlockSpec((1, t, 128), lambda i, h, j: (i, 0, h)),
                  pl.BlockSpec((1, t, B_DV), lambda i, h, j: (i, 0, OFF_BV // B_DV + h))],
        out_specs=pl.BlockSpec((1, ABLK, B_DV), lambda i, h, j: (i, j, h)),
        out_shape=jax.ShapeDtypeStruct((b, t, HEADS * B_DV), F32),
        scratch_shapes=[pltpu.VMEM((2, ABLK, 128), F32), pltpu.VMEM((2, ABLK, 2 * B_DV), F32)],
        compiler_params=_params("parallel", "parallel", "arbitrary"),
        name="attn_prompt",
    )(lam_p, gb, qn, kn, z3)


def _hgrn_kernel(lb_ref, gn_ref, e_ref, z_ref, y_ref, st_ref, s_sc, *, layer, l_true):
    tb = pl.program_id(1)

    @pl.when(tb == 0)
    def _():
        s_sc[...] = jnp.zeros_like(s_sc)

    rows = range(z_ref.shape[0])
    pre = [_hgrn_prep(r, tb, lb_ref, z_ref, layer, l_true) for r in rows]

    diag = _div(_iota((HW, HW), 0), HD) == _div(_iota((HW, HW), 1), HD)
    outs = [[] for _ in rows]
    for j in range(BLK // CHUNK):
        sl = slice(j * CHUNK, (j + 1) * CHUNK)
        for r in rows:
            c = pre[r]
            st = s_sc[r]
            outs[r].append(_bdot_nt(c["qs"][sl], st))
            dec = jnp.exp(c["gt"][j * CHUNK:j * CHUNK + 1, :])
            s_sc[r] = st * dec + jnp.where(diag, _bdot_tn(c["v"][sl], c["kd"][sl]), 0.0)

    for r in rows:
        o = pre[r]["o"] + jnp.concatenate(outs[r], axis=0)
        ss = _xdot_r(o * o, e_ref[...], 2)
        y_ref[r] = o * lax.rsqrt(ss * (1.0 / HD) + EPS) * gn_ref[...] * _silu(pre[r]["ag"])

    @pl.when(tb == pl.num_programs(1) - 1)
    def _():
        st_ref[...] = s_sc[...]


def _hgrn_prep(r, tb, lb_ref, z_ref, layer, l_true):
    z = z_ref[r]
    aq, af = z[:, 0:HW], z[:, HW:2 * HW]
    v, ag = z[:, 2 * HW:3 * HW], z[:, 3 * HW:4 * HW]
    lb = _lower_bound(lb_ref[...], layer)
    f = lb + (1.0 - lb) * _sigmoid(af)
    valid = (tb * BLK + _iota((BLK, 1), 0)) < l_true
    k = jnp.where(valid, 1.0 - f, 0.0)
    g = jnp.where(valid, jnp.log(f), 0.0)
    q = _silu(aq)

    same, tri, _ = _chunk_masks(BLK)
    gc = _xdot_l(_as01(tri), g)
    gt = _xdot_l(_as01(same), g)
    half = 0.5 * gt
    qg = q * jnp.exp(gc - half)
    kg = (k * jnp.exp(half - gc)).astype(BF16)
    kd = k * jnp.exp(gt - gc)
    qs = q * jnp.exp(gc)
    vb = v.astype(BF16)

    head = _div(_iota((1, HW), 1), HD)
    o = jnp.zeros((BLK, HW), F32)
    for h in range(HEADS):
        hm = head == h
        att = _bdot_nt(jnp.where(hm, qg, 0.0), kg)
        att = jnp.where(tri, att, 0.0)
        o = o + jnp.where(hm, jnp.dot(att.astype(BF16), vb, preferred_element_type=F32), 0.0)
    return dict(o=o, qs=qs, v=v, kd=kd, gt=gt, ag=ag)


def _hgrn_prompt(lb_p, gn, e256, z3, layer, l_true):
    b, t, _ = z3.shape
    nr = NROW if b % NROW == 0 else 1
    kern = functools.partial(_hgrn_kernel, layer=layer, l_true=l_true)
    const = lambda shape: pl.BlockSpec(shape, lambda i, j: (0,) * len(shape))
    return pl.pallas_call(
        kern,
        grid=(b // nr, t // BLK),
        in_specs=[const(lb_p.shape), const((1, HW)), const((HW, HW)),
                  pl.BlockSpec((nr, BLK, 4 * HW), lambda i, j: (i, j, OFF_A // (4 * HW)))],
        out_specs=[pl.BlockSpec((nr, BLK, HW), lambda i, j: (i, j, 0)),
                   pl.BlockSpec((nr, HW, HW), lambda i, j: (i, 0, 0))],
        out_shape=[jax.ShapeDtypeStruct((b, t, HW), F32),
                   jax.ShapeDtypeStruct((b, HW, HW), F32)],
        scratch_shapes=[pltpu.VMEM((nr, HW, HW), F32)],
        compiler_params=_params("parallel", "arbitrary"),
        name="hgrn_prompt",
    )(lb_p, gn, e256, z3)


def _delta_kernel(cw_ref, alog_ref, dtb_ref, gn_ref, e_ref, zc_ref, zg_ref, zs_ref,
                  y_ref, st_ref, s_sc, xe_sc, *, l_true):
    tb = pl.program_id(1)

    @pl.when(tb == 0)
    def _():
        s_sc[...] = jnp.zeros_like(s_sc)
        xe_sc[:, 0:8, :] = jnp.zeros((xe_sc.shape[0], 8, C_CONV), F32)

    rows = range(zc_ref.shape[0])
    units = [(r, h) for r in rows for h in range(HEADS)]
    _, tri, strict = _chunk_masks(BLK)
    head = _div(_iota((1, HW), 1), HD)
    pre = [_delta_prep(r, tb, cw_ref, alog_ref, dtb_ref, e_ref, zc_ref, zs_ref, xe_sc, l_true) for r in rows]

    eye = jnp.where(_iota((BLK, BLK), 0) == _iota((BLK, BLK), 1), 1.0, 0.0)
    inv, pw, qks = {}, {}, {}
    for r, h in units:
        c = pre[r]
        d = c["gc"][:, h * HD:h * HD + 1] - c["gc_t"][h:h + 1, :]
        rel = jnp.where(tri, jnp.exp(jnp.where(tri, d, 0.0)), 0.0)
        kq = jnp.concatenate([jnp.where(head == h, c["k"], 0.0), jnp.where(head == h, c["q"], 0.0)], axis=0)
        kq = _bdot_nt(kq, c["kb"])
        a = jnp.where(strict, c["beta"][:, h * HD:h * HD + 1] * rel * kq[0:BLK], 0.0)
        qks[r, h] = rel * kq[BLK:2 * BLK]
        inv[r, h] = eye - a
        pw[r, h] = a
    levels = int(math.log2(CHUNK)) - 1
    for u in units:
        pw[u] = _bdot_rows([pw[u]], pw[u])[0]
    for lv in range(levels):
        for u in units:
            if lv + 1 < levels:
                step, pw[u] = _bdot_rows([inv[u], pw[u]], pw[u])
            else:
                step = _bdot_rows([inv[u]], pw[u])[0]
            inv[u] = inv[u] + step
    w_all = [jnp.zeros((BLK, HW), F32) for _ in rows]
    u_all = [jnp.zeros((BLK, HW), F32) for _ in rows]
    for r, h in units:
        sol = _bdot_rows([inv[r, h]], pre[r]["rhs"])[0]
        w_all[r] = w_all[r] + jnp.where(head == h, sol[:, 0:HW], 0.0)
        u_all[r] = u_all[r] + jnp.where(head == h, sol[:, HW:2 * HW], 0.0)

    diag = _div(_iota((HW, HW), 0), HD) == _div(_iota((HW, HW), 1), HD)
    outs = [[] for _ in rows]
    us = [[] for _ in rows]
    for j in range(BLK // CHUNK):
        sl = slice(j * CHUNK, (j + 1) * CHUNK)
        for r in rows:
            c = pre[r]
            st = s_sc[r]
            wq = jnp.concatenate([w_all[r][sl], c["qg"][sl]], axis=0)
            ws = jnp.dot(wq.astype(BF16), st.astype(BF16), preferred_element_type=F32)
            u = u_all[r][sl] - ws[0:CHUNK]
            outs[r].append(ws[CHUNK:2 * CHUNK])
            us[r].append(u)
            dec = jnp.exp(c["gt"][j * CHUNK:j * CHUNK + 1, :])
            s_sc[r] = st * dec + jnp.where(diag, _bdot_tn(c["kd"][sl], u), 0.0)

    for r in rows:
        o = jnp.concatenate(outs[r], axis=0)
        ub = jnp.concatenate(us[r], axis=0).astype(BF16)
        qu = jnp.dot(jnp.concatenate([qks[r, h].astype(BF16) for h in range(HEADS)], axis=0), ub,
                     preferred_element_type=F32)
        for h in range(HEADS):
            o = o + jnp.where(head == h, qu[h * BLK:(h + 1) * BLK], 0.0)
        ss = _xdot_r(o * o, e_ref[...], 2)
        y_ref[r] = o * lax.rsqrt(ss * (1.0 / HD) + EPS) * gn_ref[...] * _silu(zg_ref[r])

    @pl.when(tb == pl.num_programs(1) - 1)
    def _():
        st_ref[...] = s_sc[...]


def _delta_prep(r, tb, cw_ref, alog_ref, dtb_ref, e_ref, zc_ref, zs_ref, xe_sc, l_true):
    x = zc_ref[r]
    xe_sc[r, 8:8 + BLK, :] = x
    cw = cw_ref[...]
    conv = jnp.zeros((BLK, C_CONV), F32)
    for w in range(CONV_W):
        start = 8 - (CONV_W - 1) + w
        conv = conv + xe_sc[r, start:start + BLK, :] * cw[w:w + 1, :]
    xe_sc[r, 0:8, :] = x[BLK - 8:BLK, :]
    act = _silu(conv)
    qc, kc, v = act[:, 0:HW], act[:, HW:2 * HW], act[:, 2 * HW:3 * HW]
    e = e_ref[...]
    q = qc * lax.rsqrt(_xdot_r(qc * qc, e, 2) + EPS) * (HD ** -0.5)
    k = kc * lax.rsqrt(_xdot_r(kc * kc, e, 2) + EPS)

    zs = zs_ref[r]
    valid = (tb * BLK + _iota((BLK, 1), 0)) < l_true
    beta_s = jnp.where(valid, _sigmoid(zs), 0.0)
    la_s = jnp.where(valid, -jnp.exp(alog_ref[...]) * _softplus(zs + dtb_ref[...]), 0.0)
    src = _iota((128, HW), 0)
    dst_head = _div(_iota((128, HW), 1), HD)
    beta = _xdot_r(beta_s, _as01(src == dst_head))
    la = _xdot_r(la_s, _as01(src == dst_head + HEADS))

    same, tri, strict = _chunk_masks(BLK)
    gc = _xdot_l(_as01(tri), la)
    gt = _xdot_l(_as01(same), la)
    sel = _as01((_iota((8, HW), 1) == _iota((8, HW), 0) * HD) & (_iota((8, HW), 0) < HEADS))
    gc_t = _xdot_nt(sel, gc)
    eg = jnp.exp(gc)
    return dict(gc=gc, gt=gt, gc_t=gc_t, k=k, kb=k.astype(BF16), q=q, beta=beta,
                kd=k * jnp.exp(gt - gc), qg=q * eg,
                rhs=jnp.concatenate([beta * eg * k, beta * v], axis=1))


def _delta_prompt(cw, alog_p, dtb_p, gn, e256, z3, l_true):
    b, t, _ = z3.shape
    nr = NROW if b % NROW == 0 else 1
    kern = functools.partial(_delta_kernel, l_true=l_true)
    const = lambda shape: pl.BlockSpec(shape, lambda i, j: (0,) * len(shape))
    return pl.pallas_call(
        kern,
        grid=(b // nr, t // BLK),
        in_specs=[const((CONV_W, C_CONV)), const((1, 128)), const((1, 128)), const((1, HW)),
                  const((HW, HW)),
                  pl.BlockSpec((nr, BLK, C_CONV), lambda i, j: (i, j, OFF_CQKV // C_CONV)),
                  pl.BlockSpec((nr, BLK, HW), lambda i, j: (i, j, OFF_CG // HW)),
                  pl.BlockSpec((nr, BLK, 128), lambda i, j: (i, j, OFF_CS // 128))],
        out_specs=[pl.BlockSpec((nr, BLK, HW), lambda i, j: (i, j, 0)),
                   pl.BlockSpec((nr, HW, HW), lambda i, j: (i, 0, 0))],
        out_shape=[jax.ShapeDtypeStruct((b, t, HW), F32),
                   jax.ShapeDtypeStruct((b, HW, HW), F32)],
        scratch_shapes=[pltpu.VMEM((nr, HW, HW), F32), pltpu.VMEM((nr, 8 + BLK, C_CONV), F32)],
        compiler_params=_params("parallel", "arbitrary"),
        name="delta_prompt",
    )(cw, alog_p, dtb_p, gn, e256, z3, z3, z3)


def _sample_rec_kernel(lb_ref, gna_ref, gnc_ref, cw_ref, alog_ref, dtb_ref, z_ref, hst_ref, dst_ref,
                       cst_ref, ya_ref, yc_ref, hso_ref, dso_ref, *, layer):
    z = z_ref[0]
    eye = _iota((HW, HW), 0) == _iota((HW, HW), 1)

    def col(r):
        return jnp.sum(jnp.where(eye, r, 0.0), axis=1, keepdims=True)

    def head_norm(o, g, gate):
        return o * lax.rsqrt(jnp.mean(o * o, axis=-1, keepdims=True) + EPS) * g * _silu(gate)

    za = z[:, OFF_A:OFF_A + 4 * HW]
    aq, af = za[:, 0:HW], za[:, HW:2 * HW]
    v, ag = za[:, 2 * HW:3 * HW], za[:, 3 * HW:4 * HW]
    lb = _lower_bound(lb_ref[...], layer)
    f = lb + (1.0 - lb) * _sigmoid(af)
    fcol, kcol, qcol = col(f), col(1.0 - f), col(_silu(aq))
    for h in range(HEADS):
        sl = slice(h * HD, (h + 1) * HD)
        s = fcol[sl] * hst_ref[0, h] + kcol[sl] * v[:, sl]
        hso_ref[0, h] = s
        o = jnp.sum(qcol[sl] * s, axis=0, keepdims=True)
        ya_ref[0, :, sl] = head_norm(o, gna_ref[...], ag[:, sl])

    cst = cst_ref[0]
    cw = cw_ref[...]
    conv = z[:, OFF_CQKV:OFF_CQKV + C_CONV] * cw[CONV_W - 1:CONV_W]
    for w in range(CONV_W - 1):
        conv = conv + cst[w:w + 1] * cw[w:w + 1]
    act = _silu(conv)
    qc, kc, vc = act[:, 0:HW], act[:, HW:2 * HW], act[:, 2 * HW:3 * HW]
    cg = z[:, OFF_CG:OFF_CG + HW]
    zs = z[:, OFF_CS:OFF_CS + 128]
    beta_all = _sigmoid(zs)
    la_all = -jnp.exp(alog_ref[...]) * _softplus(zs + dtb_ref[...])
    qcol, kcol = col(qc), col(kc)
    for h in range(HEADS):
        sl = slice(h * HD, (h + 1) * HD)
        rq = lax.rsqrt(jnp.sum(qc[:, sl] * qc[:, sl], axis=-1, keepdims=True) + EPS) * (HD ** -0.5)
        rk = lax.rsqrt(jnp.sum(kc[:, sl] * kc[:, sl], axis=-1, keepdims=True) + EPS)
        kh, qh = kcol[sl] * rk, qcol[sl] * rq
        s = dst_ref[0, h] * jnp.exp(la_all[:, HEADS + h:HEADS + h + 1])
        u = beta_all[:, h:h + 1] * (vc[:, sl] - jnp.sum(kh * s, axis=0, keepdims=True))
        s = s + kh * u
        dso_ref[0, h] = s
        o = jnp.sum(qh * s, axis=0, keepdims=True)
        yc_ref[0, :, sl] = head_norm(o, gnc_ref[...], cg[:, sl])


def _sample_rec(lb_p, gna, gnc, cw, alog_p, dtb_p, zs3, hst, dst, cst, layer):
    nb = zs3.shape[0]
    kern = functools.partial(_sample_rec_kernel, layer=layer)
    const = lambda shape: pl.BlockSpec(shape, lambda i: (0,) * len(shape))
    st = pl.BlockSpec((1, HEADS, HD, HD), lambda i: (i, 0, 0, 0))
    yo = pl.BlockSpec((1, 1, HW), lambda i: (i, 0, 0))
    return pl.pallas_call(
        kern,
        grid=(nb,),
        in_specs=[const(lb_p.shape), const((1, HD)), const((1, HD)), const((CONV_W, C_CONV)),
                  const((1, 128)), const((1, 128)),
                  pl.BlockSpec((1, 1, NZ), lambda i: (i, 0, 0)), st, st,
                  pl.BlockSpec((1, CONV_W - 1, C_CONV), lambda i: (i, 0, 0))],
        out_specs=[yo, yo, st, st],
        out_shape=[jax.ShapeDtypeStruct((nb, 1, HW), F32)] * 2
                  + [jax.ShapeDtypeStruct((nb, HEADS, HD, HD), F32)] * 2,
        compiler_params=_params("parallel"),
        name="sample_rec",
    )(lb_p, gna, gnc, cw, alog_p, dtb_p, zs3, hst, dst, cst)


def _paged_kernel(pt_ref, lam_ref, gb_ref, q_ref, kn_ref, vn_ref, *rest, pps, lam_init, b_scale):
    k_refs, v_refs = rest[:pps], rest[pps:2 * pps]
    o_ref, m_sc, l_sc, acc_sc = rest[2 * pps:]
    s_id = pl.program_id(1)
    rows = k_refs[0].shape[0]
    rhead = _iota((8, 128), 0) & (HEADS - 1)
    lane_comp = _div(_iota((8, 128), 1), HD)

    def by_head(r):
        out = jnp.zeros((8, 128), F32)
        for h in range(HEADS):
            out = jnp.where(rhead == h, r[:, h * 128:(h + 1) * 128], out)
        return out

    def merge(x):
        return pltpu.roll(x, HEADS, 0)

    def widen(x, c):
        return jnp.where(lane_comp == c, x, pltpu.roll(x, HD, 1))

    @pl.when(s_id == 0)
    def _():
        m_sc[...] = jnp.full_like(m_sc, -jnp.inf)
        l_sc[...] = jnp.zeros_like(l_sc)
        acc_sc[...] = jnp.zeros_like(acc_sc)

    q8 = by_head(q_ref[0])
    comp_sum = _as01(_div(_iota((128, 128), 0), HD) == _div(_iota((128, 128), 1), HD))
    spread = _as01(_iota((128, 256), 0) == _div(_iota((128, 256), 1), 128) * HD)

    def scores(i):
        prod = (k_refs[i][...].reshape(rows // 8, 8, 128) * q8).reshape(rows, 128)
        return jnp.dot(prod.astype(BF16), comp_sum, preferred_element_type=F32).reshape(rows // 8, 8, 128)

    def softmax(s3, m_old, l_old):
        mx = jnp.max(s3, axis=0)
        m_new = jnp.maximum(m_old, jnp.maximum(mx, merge(mx)))
        alpha = jnp.exp2(m_old - m_new)
        p = jnp.exp2(s3 - m_new)
        pw = jnp.dot(p.reshape(rows, 128).astype(BF16), spread, preferred_element_type=F32)
        return m_new, alpha * l_old + jnp.sum(p, axis=0), alpha, pw

    def weigh(i, alpha, pw, acc):
        v3 = v_refs[i][...].reshape(rows // 8, 8, B_DV)
        return [widen(alpha, c) * acc[c]
                + jnp.sum(pw[:, c * 128:(c + 1) * 128].reshape(rows // 8, 8, B_DV) * v3, axis=0)
                for c in range(2)]

    m, l, acc = m_sc[...], l_sc[...], [acc_sc[0], acc_sc[1]]
    s_q, w_q = {}, {}
    for t in range(pps + 2):
        if t < pps:
            s_q[t] = scores(t)
        if 1 <= t <= pps:
            m, l, alpha, pw = softmax(s_q.pop(t - 1), m, l)
            w_q[t - 1] = (alpha, pw)
        if t >= 2:
            acc = weigh(t - 2, *w_q.pop(t - 2), acc)
    m_sc[...] = m
    l_sc[...] = l
    acc_sc[0] = acc[0]
    acc_sc[1] = acc[1]

    @pl.when(s_id == pl.num_programs(1) - 1)
    def _():
        lam = _lambda(lam_ref[...], lam_init)
        qk_new = q8 * by_head(kn_ref[0])
        v_new = by_head(vn_ref[0])
        s_new = jnp.zeros((8, 128), F32)
        for c in range(2):
            s_new = jnp.where(lane_comp == c,
                              jnp.sum(jnp.where(lane_comp == c, qk_new, 0.0), axis=1, keepdims=True), s_new)
        m_old = m_sc[...]
        m_fin = jnp.maximum(m_old, s_new)
        alpha = jnp.exp2(m_old - m_fin)
        p_new = jnp.exp2(s_new - m_fin)
        l_fin = alpha * (l_sc[...] + merge(l_sc[...])) + p_new
        outs = []
        for c in range(2):
            a_fin = widen(alpha, c) * (acc_sc[c] + merge(acc_sc[c])) + widen(p_new, c) * v_new
            outs.append(a_fin / widen(l_fin, c))
        o = outs[0] - lam * outs[1]
        y = o * lax.rsqrt(jnp.mean(o * o, axis=-1, keepdims=True) + EPS) * gb_ref[...] * b_scale
        for h in range(HEADS):
            o_ref[0, :, h * B_DV:(h + 1) * B_DV] = y[h:h + 1]


def _paged_attn(page_table, lam_p, gb, qs3, kn3, vn3, cache_k, cache_v, layer, lam_init, b_scale, pps):
    nb, n_pages = page_table.shape
    depth, n_pool, page, nh, dd = cache_k.shape
    assert nh == HEADS and dd == 128 and cache_v.shape == cache_k.shape
    cache_k = cache_k.reshape(depth, n_pool, page * nh, dd)
    cache_v = cache_v.reshape(depth, n_pool, page * nh, dd)
    kern = functools.partial(_paged_kernel, pps=pps, lam_init=lam_init, b_scale=b_scale)
    row = pl.BlockSpec((1, 1, 2 * HW), lambda b, s, pt: (b, 0, 0))

    def page_spec(i):
        return pl.BlockSpec((None, None, page * nh, dd),
                            lambda b, s, pt: (layer, pt[b, s * pps + i], 0, 0))

    grid_spec = pltpu.PrefetchScalarGridSpec(
        num_scalar_prefetch=1,
        grid=(nb, n_pages // pps),
        in_specs=[pl.BlockSpec(lam_p.shape, lambda b, s, pt: (0, 0)),
                  pl.BlockSpec((1, B_DV), lambda b, s, pt: (0, 0)),
                  row, row, row]
                 + [page_spec(i) for i in range(pps)] * 2,
        out_specs=row,
        scratch_shapes=[pltpu.VMEM((8, 128), F32), pltpu.VMEM((8, 128), F32), pltpu.VMEM((2, 8, B_DV), F32)],
    )
    return pl.pallas_call(
        kern,
        grid_spec=grid_spec,
        out_shape=jax.ShapeDtypeStruct((nb, 1, 2 * HW), F32),
        compiler_params=_params("parallel", "arbitrary"),
        name="paged_attn",
    )(page_table, lam_p, gb, qs3, kn3, vn3, *([cache_k] * pps), *([cache_v] * pps))


def _diag_blocks(s, transpose):
    blocks = jnp.stack([s[:, h * HD:(h + 1) * HD, h * HD:(h + 1) * HD] for h in range(HEADS)], axis=1)
    return jnp.swapaxes(blocks, -1, -2) if transpose else blocks


def kernel(x_prompt, x_sample, cache_k, cache_v, page_table, state_hgrn, state_delta, state_conv,
           meta_tokens, norm_mix, norm_ffn, w_in, w_out, hgrn_lower_bound, hgrn_out_norm,
           diff_q_norm, diff_k_norm, diff_lambda, diff_out_norm, conv_w, delta_a_log,
           delta_dt_bias, delta_out_norm, w_up, w_down):
    nb_p, seq, d_model = x_prompt.shape
    nb_s = x_sample.shape[0]
    depth = w_in.shape[0]
    l_true = N_META + seq
    l_pad = -(-l_true // ABLK) * ABLK
    assert ABLK % BLK == 0 and x_sample.shape[1] == 1
    n_past = page_table.shape[1] * cache_k.shape[2]

    xp = jnp.concatenate([jnp.broadcast_to(meta_tokens[None].astype(F32), (nb_p, N_META, d_model)),
                          x_prompt, jnp.zeros((nb_p, l_pad - l_true, d_model), F32)], axis=1)
    xp = xp.reshape(nb_p * l_pad, d_model)
    xs = x_sample.reshape(nb_s, d_model)

    e256 = jnp.kron(jnp.eye(HEADS, dtype=F32), jnp.ones((HD, HD), F32)).astype(BF16)
    e512 = jnp.kron(jnp.eye(2 * HEADS, dtype=F32), jnp.ones((HD, HD), F32)).astype(BF16)
    tabs_p = _rope_tables(jnp.arange(l_pad))
    tabs_s = _rope_tables(jnp.full((nb_s,), n_past))
    w_in_p = jnp.concatenate([w_in[:, :, 2560:3584], w_in[:, :, 0:2560], w_in[:, :, 3584:D_IN],
                              jnp.zeros(w_in.shape[:2] + (NZ - D_IN,), w_in.dtype)], axis=2).astype(BF16)
    w_out_b, w_up_b, w_down_b = w_out.astype(BF16), w_up.astype(BF16), w_down.astype(BF16)
    lane_pad = lambda a: jnp.pad(a[None, :], ((0, 0), (HEADS, 128 - 2 * HEADS)))

    outs = {k: [] for k in ("kp", "vp", "ks", "vs", "hp", "hs", "dp", "ds", "cp", "cs")}
    for l in range(depth):
        lam_init = 0.8 - 0.6 * math.exp(-0.3 * l)
        b_scale = 1.0 - lam_init
        g_mix, g_ffn = norm_mix[l][None], norm_ffn[l][None]
        gq, gk = jnp.tile(diff_q_norm[l], 2 * HEADS)[None], jnp.tile(diff_k_norm[l], 2 * HEADS)[None]
        gna, gnc = hgrn_out_norm[l][None], delta_out_norm[l][None]
        gna_t, gnc_t = jnp.tile(gna, (1, HEADS)), jnp.tile(gnc, (1, HEADS))
        gb = diff_out_norm[l][None]
        alog_p, dtb_p = lane_pad(delta_a_log[l]), lane_pad(delta_dt_bias[l])

        z3 = _proj(xp, g_mix, w_in_p[l], _row_tile(xp.shape[0], 256)).reshape(nb_p, l_pad, NZ)
        ya, st_a = _hgrn_prompt(hgrn_lower_bound, gna_t, e256, z3, l, l_true)
        qn, kn = _qkrope(z3, gq, gk, e512, tabs_p, BLK)
        yb = _attn_prompt(diff_lambda[l], gb, qn, kn, z3, lam_init, b_scale)
        yc, st_c = _delta_prompt(conv_w[l], alog_p, dtb_p, gnc_t, e256, z3, l_true)
        rows = nb_p * l_pad
        xp = _merge_ffn(xp, ya.reshape(rows, -1), yb.reshape(rows, -1), yc.reshape(rows, -1),
                        w_out_b[l], g_ffn, w_up_b[l], w_down_b[l], _row_tile(rows, 512), 1024)
        outs["kp"].append(kn[:, :l_true].reshape(nb_p, l_true, HEADS, 2 * HD))
        outs["vp"].append(z3[:, :l_true, OFF_BV:OFF_BV + HEADS * B_DV].reshape(nb_p, l_true, HEADS, B_DV))
        outs["hp"].append(_diag_blocks(st_a, True))
        outs["dp"].append(_diag_blocks(st_c, False))
        outs["cp"].append(z3[:, l_true - (CONV_W - 1):l_true, OFF_CQKV:OFF_CQKV + C_CONV])

        zs = _proj(xs, g_mix, w_in_p[l], nb_s)
        zs3 = zs.reshape(nb_s, 1, NZ)
        ya_s, yc_s, hs_new, ds_new = _sample_rec(hgrn_lower_bound, gna, gnc, conv_w[l], alog_p, dtb_p,
                                                 zs3, state_hgrn[l], state_delta[l], state_conv[l], l)
        qn_s, kn_s = _qkrope(zs.reshape(1, nb_s, NZ), gq, gk, e512, tabs_s, nb_s)
        vn_s = zs[:, OFF_BV:OFF_BV + HEADS * B_DV]
        yb_s = _paged_attn(page_table, diff_lambda[l], gb, qn_s.reshape(nb_s, 1, -1),
                           kn_s.reshape(nb_s, 1, -1), vn_s.reshape(nb_s, 1, -1),
                           cache_k, cache_v, l, lam_init, b_scale, PAGES_PER_STEP)
        xs = _merge_ffn(xs, ya_s.reshape(nb_s, -1), yb_s.reshape(nb_s, -1), yc_s.reshape(nb_s, -1),
                        w_out_b[l], g_ffn, w_up_b[l], w_down_b[l], nb_s, 1024)
        outs["ks"].append(kn_s.reshape(nb_s, 1, HEADS, 2 * HD))
        outs["vs"].append(vn_s.reshape(nb_s, 1, HEADS, B_DV))
        outs["hs"].append(hs_new)
        outs["ds"].append(ds_new)
        outs["cs"].append(jnp.concatenate([state_conv[l][:, 1:], zs[:, None, OFF_CQKV:OFF_CQKV + C_CONV]],
                                          axis=1))

    y_prompt = xp.reshape(nb_p, l_pad, d_model)[:, N_META:l_true]
    y_sample = xs.reshape(nb_s, 1, d_model)
    return (y_prompt, y_sample) + tuple(jnp.stack(outs[k]) for k in
                                        ("kp", "vp", "ks", "vs", "hp", "hs", "dp", "ds", "cp", "cs"))
```

```python
import functools
import math

import jax
import jax.numpy as jnp
from jax import lax
from jax.experimental import pallas as pl
from jax.experimental.pallas import tpu as pltpu

F32 = jnp.float32
BF16 = jnp.bfloat16

EPS = 1e-6
N_META = 16
CHUNK = 16
BLK = 128
ABLK = 768
NROW = 2
PAGES_PER_STEP = 16
HEADS = 4
HD = 64
HW = HEADS * HD
B_DV = 128
ROT_DIM = 16
ROPE_THETA = 500000.0
Q_SCALE = HD ** -0.5 * math.log2(math.e)
CONV_W = 4
C_CONV = 3 * HW
D_IN = 3592
NZ = 3712
OFF_CQKV, OFF_CG = 0, 768
OFF_A = 1024
OFF_BQ, OFF_BK, OFF_BV = 2048, 2560, 3072
OFF_CS = 3584

VMEM_LIMIT = 48 * 1024 * 1024


def _bdot(a, b):
    return jnp.dot(a.astype(BF16), b.astype(BF16), preferred_element_type=F32)


def _bdot_nt(a, b):
    return lax.dot_general(a.astype(BF16), b.astype(BF16), (((1,), (1,)), ((), ())),
                           preferred_element_type=F32)


def _bdot_tn(a, b):
    return lax.dot_general(a.astype(BF16), b.astype(BF16), (((0,), (0,)), ((), ())),
                           preferred_element_type=F32)


def _split(a, n):
    parts, r = [], a
    for i in range(n):
        p = r.astype(BF16)
        parts.append(p)
        if i + 1 < n:
            r = r - p.astype(F32)
    return parts


def _xdot_l(m01, a, n=3):
    return sum(jnp.dot(m01, p, preferred_element_type=F32) for p in _split(a, n))


def _xdot_r(a, m01, n=3):
    return sum(jnp.dot(p, m01, preferred_element_type=F32) for p in _split(a, n))


def _xdot_nt(m01, a, n=3):
    return sum(lax.dot_general(m01, p, (((1,), (1,)), ((), ())), preferred_element_type=F32)
               for p in _split(a, n))


def _bdot_rows(lhs, b):
    m = lhs[0].shape[0]
    r = jnp.dot(jnp.concatenate([l.astype(BF16) for l in lhs], axis=0), b.astype(BF16),
                preferred_element_type=F32)
    return [r[i * m:(i + 1) * m] for i in range(len(lhs))]


def _sigmoid(x):
    return 1.0 / (1.0 + jnp.exp(-x))


def _silu(x):
    return x * _sigmoid(x)


def _softplus(x):
    return jnp.maximum(x, 0.0) + jnp.log(1.0 + jnp.exp(-jnp.abs(x)))


def _iota(shape, dim):
    return lax.broadcasted_iota(jnp.int32, shape, dim)


def _div(x, n):
    return lax.shift_right_logical(x, int(math.log2(n)))


def _chunk_masks(n):
    r, c = _iota((n, n), 0), _iota((n, n), 1)
    same = _div(r, CHUNK) == _div(c, CHUNK)
    return same, same & (c <= r), same & (c < r)


def _as01(mask):
    return jnp.where(mask, 1.0, 0.0).astype(BF16)


def _lower_bound(p, layer):
    mx = jnp.max(p, axis=0, keepdims=True)
    ex = jnp.exp(p - mx)
    soft = ex / jnp.sum(ex, axis=0, keepdims=True)
    if layer == 0:
        return jnp.zeros_like(soft[0:1])
    return jnp.sum(soft[1:layer + 1], axis=0, keepdims=True)


def _lambda(lv, lam_init):
    a = jnp.sum(lv[0:1] * lv[1:2], axis=1, keepdims=True)
    b = jnp.sum(lv[2:3] * lv[3:4], axis=1, keepdims=True)
    return jnp.exp(a) - jnp.exp(b) + lam_init


def _row_tile(m, want):
    t = min(want, m)
    while m % t:
        t //= 2
    assert t == m or t % 8 == 0, (m, want)
    return t


def _params(*sem):
    return pltpu.CompilerParams(dimension_semantics=sem, vmem_limit_bytes=VMEM_LIMIT)


def _proj_kernel(x_ref, g_ref, w_ref, o_ref):
    x = x_ref[...]
    h = x * lax.rsqrt(jnp.mean(x * x, axis=-1, keepdims=True) + EPS) * g_ref[...]
    o_ref[...] = jnp.dot(h.astype(BF16), w_ref[...], preferred_element_type=F32)


def _proj(x, g, w, tm):
    m, d = x.shape
    n = w.shape[1]
    return pl.pallas_call(
        _proj_kernel,
        grid=(m // tm,),
        in_specs=[pl.BlockSpec((tm, d), lambda i: (i, 0)),
                  pl.BlockSpec((1, d), lambda i: (0, 0)),
                  pl.BlockSpec((d, n), lambda i: (0, 0))],
        out_specs=pl.BlockSpec((tm, n), lambda i: (i, 0)),
        out_shape=jax.ShapeDtypeStruct((m, n), F32),
        compiler_params=_params("parallel"),
        name="proj",
    )(x, g, w)


def _ffn_kernel(x_ref, ya_ref, yb_ref, yc_ref, wo_ref, g_ref, wu_ref, wd_ref, o_ref, h_sc, acc_sc):
    k = pl.program_id(1)

    @pl.when(k == 0)
    def _():
        mix = jnp.concatenate([ya_ref[...], yb_ref[...], yc_ref[...]], axis=-1)
        x1 = x_ref[...] + jnp.dot(mix.astype(BF16), wo_ref[...], preferred_element_type=F32)
        h = x1 * lax.rsqrt(jnp.mean(x1 * x1, axis=-1, keepdims=True) + EPS) * g_ref[...]
        h_sc[...] = h.astype(BF16)
        acc_sc[...] = x1

    hid = jnp.dot(h_sc[...], wu_ref[...], preferred_element_type=F32)
    hid = jnp.square(jnp.maximum(hid, 0.0))
    acc_sc[...] += jnp.dot(hid.astype(BF16), wd_ref[...], preferred_element_type=F32)

    @pl.when(k == pl.num_programs(1) - 1)
    def _():
        o_ref[...] = acc_sc[...]


def _merge_ffn(x, ya, yb, yc, wo, g, wu, wd, tm, tf):
    m, d = x.shape
    dff = wu.shape[1]
    row = lambda i, k: (i, 0)
    return pl.pallas_call(
        _ffn_kernel,
        grid=(m // tm, dff // tf),
        in_specs=[pl.BlockSpec((tm, d), row),
                  pl.BlockSpec((tm, ya.shape[1]), row),
                  pl.BlockSpec((tm, yb.shape[1]), row),
                  pl.BlockSpec((tm, yc.shape[1]), row),
                  pl.BlockSpec(wo.shape, lambda i, k: (0, 0)),
                  pl.BlockSpec((1, d), lambda i, k: (0, 0)),
                  pl.BlockSpec((d, tf), lambda i, k: (0, k)),
                  pl.BlockSpec((tf, d), lambda i, k: (k, 0))],
        out_specs=pl.BlockSpec((tm, d), row),
        out_shape=jax.ShapeDtypeStruct((m, d), F32),
        scratch_shapes=[pltpu.VMEM((tm, d), BF16), pltpu.VMEM((tm, d), F32)],
        compiler_params=_params("parallel", "arbitrary"),
        name="merge_ffn",
    )(x, ya, yb, yc, wo, g, wu, wd)


def _qkrope_kernel(gq_ref, gk_ref, e_ref, c_ref, s1_ref, s2_ref, q_ref, k_ref, qo_ref, ko_ref):
    e = e_ref[...]
    cos, s_next, s_prev = c_ref[...], s1_ref[...], s2_ref[...]

    def norm_rope(x, g, scale):
        ss = _xdot_r(x * x, e, 2)
        xn = x * lax.rsqrt(ss * (1.0 / HD) + EPS) * g
        outs = []
        for h in range(HEADS):
            s = xn[:, h * 128:(h + 1) * 128]
            r = (s * cos + pltpu.roll(s, 128 - ROT_DIM // 2, 1) * s_next
                 + pltpu.roll(s, ROT_DIM // 2, 1) * s_prev)
            outs.append(r * scale if scale != 1.0 else r)
        return jnp.concatenate(outs, axis=1)

    qo_ref[0] = norm_rope(q_ref[0], gq_ref[...], Q_SCALE)
    ko_ref[0] = norm_rope(k_ref[0], gk_ref[...], 1.0)


def _qkrope(z3, gq, gk, e512, tabs, tt):
    b, t, _ = z3.shape
    w = 2 * HW
    const = lambda shape: pl.BlockSpec(shape, lambda i, j: (0,) * len(shape))
    tab = pl.BlockSpec((tt, 128), lambda i, j: (j, 0))
    out = pl.BlockSpec((1, tt, w), lambda i, j: (i, j, 0))
    return pl.pallas_call(
        _qkrope_kernel,
        grid=(b, t // tt),
        in_specs=[const((1, w)), const((1, w)), const((w, w)), tab, tab, tab,
                  pl.BlockSpec((1, tt, w), lambda i, j: (i, j, OFF_BQ // w)),
                  pl.BlockSpec((1, tt, w), lambda i, j: (i, j, OFF_BK // w))],
        out_specs=[out, out],
        out_shape=[jax.ShapeDtypeStruct((b, t, w), F32)] * 2,
        compiler_params=_params("parallel", "parallel"),
        name="qkrope",
    )(gq, gk, e512, *tabs, z3, z3)


def _rope_tables(pos):
    half = ROT_DIM // 2
    inv = jnp.power(ROPE_THETA, -jnp.arange(half, dtype=F32) * 2.0 / ROT_DIM)
    ang = pos.astype(F32)[:, None] * inv[None, :]
    cos, sin = jnp.cos(ang), jnp.sin(ang)
    n = pos.shape[0]
    one = jnp.ones((n, HD - ROT_DIM), F32)
    zero = jnp.zeros((n, HD - half), F32)
    c = jnp.concatenate([cos, cos, one], axis=1)
    s_next = jnp.concatenate([-sin, zero], axis=1)
    s_prev = jnp.concatenate([jnp.zeros((n, half), F32), sin, one * 0.0], axis=1)
    return tuple(jnp.tile(a, (1, 2)) for a in (c, s_next, s_prev))


def _attn_kernel(lam_ref, gb_ref, q_ref, k_ref, v_ref, o_ref, m_sc, a_sc, *, lam_init, b_scale):
    i = pl.program_id(2)
    q = q_ref[0]
    lane = _iota((1, 128), 1)
    qs = (jnp.where(lane < HD, q, 0.0).astype(BF16), jnp.where(lane >= HD, q, 0.0).astype(BF16))
    m_sc[...] = jnp.full_like(m_sc, -jnp.inf)
    a_sc[...] = jnp.zeros_like(a_sc)
    ones = jnp.ones((ABLK, 128), BF16)

    def block(j, keep):
        start = pl.multiple_of(j * ABLK, ABLK)
        kb = k_ref[0, pl.ds(start, ABLK), :].astype(BF16)
        vb = jnp.concatenate([v_ref[0, pl.ds(start, ABLK), :].astype(BF16), ones], axis=1)
        scores = [lax.dot_general(qs[c], kb, (((1,), (1,)), ((), ())), preferred_element_type=F32)
                  for c in range(2)]
        for c in range(2):
            s = scores[c] if keep is None else jnp.where(keep, scores[c], -jnp.inf)
            m_old = m_sc[c]
            m_new = jnp.maximum(m_old, jnp.max(s, axis=1, keepdims=True))
            alpha = jnp.exp2(m_old - m_new)
            p = jnp.concatenate([jnp.exp2(s[:, n * 128:(n + 1) * 128] - m_new) for n in range(ABLK // 128)],
                                axis=1)
            a_sc[c] = (jnp.concatenate([alpha, alpha], axis=1) * a_sc[c]
                       + jnp.dot(p.astype(BF16), vb, preferred_element_type=F32))
            m_sc[c] = m_new

    def body(j, carry):
        block(j, None)
        return carry

    lax.fori_loop(0, i, body, 0)
    block(i, _iota((ABLK, ABLK), 1) <= _iota((ABLK, ABLK), 0))

    lam = _lambda(lam_ref[...], lam_init)
    o = (a_sc[0, :, 0:B_DV] / a_sc[0, :, B_DV:2 * B_DV]
         - lam * (a_sc[1, :, 0:B_DV] / a_sc[1, :, B_DV:2 * B_DV]))
    y = o * lax.rsqrt(jnp.mean(o * o, axis=-1, keepdims=True) + EPS) * gb_ref[...]
    o_ref[0] = y * b_scale


def _attn_prompt(lam_p, gb, qn, kn, z3, lam_init, b_scale):
    b, t, _ = qn.shape
    kern = functools.partial(_attn_kernel, lam_init=lam_init, b_scale=b_scale)
    return pl.pallas_call(
        kern,
        grid=(b, HEADS, t // ABLK),
        in_specs=[pl.BlockSpec(lam_p.shape, lambda i, h, j: (0, 0)),
                  pl.BlockSpec((1, B_DV), lambda i, h, j: (0, 0)),
                  pl.BlockSpec((1, ABLK, 128), lambda i, h, j: (i, j, h)),
                  pl.BlockSpec((1, t, 128), lambda i, h, j: (i, 0, h)),
                  pl.BlockSpec((1, t, B_DV), lambda i, h, j: (i, 0, OFF_BV // B_DV + h))],
        out_specs=pl.BlockSpec((1, ABLK, B_DV), lambda i, h, j: (i, j, h)),
        out_shape=jax.ShapeDtypeStruct((b, t, HEADS * B_DV), F32),
        scratch_shapes=[pltpu.VMEM((2, ABLK, 128), F32), pltpu.VMEM((2, ABLK, 2 * B_DV), F32)],
        compiler_params=_params("parallel", "parallel", "arbitrary"),
        name="attn_prompt",
    )(lam_p, gb, qn, kn, z3)


def _hgrn_kernel(lb_ref, gn_ref, e_ref, z_ref, y_ref, st_ref, s_sc, *, layer, l_true):
    tb = pl.program_id(1)

    @pl.when(tb == 0)
    def _():
        s_sc[...] = jnp.zeros_like(s_sc)

    rows = range(z_ref.shape[0])
    pre = [_hgrn_prep(r, tb, lb_ref, z_ref, layer, l_true) for r in rows]

    diag = _div(_iota((HW, HW), 0), HD) == _div(_iota((HW, HW), 1), HD)
    outs = [[] for _ in rows]
    for j in range(BLK // CHUNK):
        sl = slice(j * CHUNK, (j + 1) * CHUNK)
        for r in rows:
            c = pre[r]
            st = s_sc[r]
            outs[r].append(_bdot_nt(c["qs"][sl], st))
            dec = jnp.exp(c["gt"][j * CHUNK:j * CHUNK + 1, :])
            s_sc[r] = st * dec + jnp.where(diag, _bdot_tn(c["v"][sl], c["kd"][sl]), 0.0)

    for r in rows:
        o = pre[r]["o"] + jnp.concatenate(outs[r], axis=0)
        ss = _xdot_r(o * o, e_ref[...], 2)
        y_ref[r] = o * lax.rsqrt(ss * (1.0 / HD) + EPS) * gn_ref[...] * _silu(pre[r]["ag"])

    @pl.when(tb == pl.num_programs(1) - 1)
    def _():
        st_ref[...] = s_sc[...]


def _hgrn_prep(r, tb, lb_ref, z_ref, layer, l_true):
    z = z_ref[r]
    aq, af = z[:, 0:HW], z[:, HW:2 * HW]
    v, ag = z[:, 2 * HW:3 * HW], z[:, 3 * HW:4 * HW]
    lb = _lower_bound(lb_ref[...], layer)
    f = lb + (1.0 - lb) * _sigmoid(af)
    valid = (tb * BLK + _iota((BLK, 1), 0)) < l_true
    k = jnp.where(valid, 1.0 - f, 0.0)
    g = jnp.where(valid, jnp.log(f), 0.0)
    q = _silu(aq)

    same, tri, _ = _chunk_masks(BLK)
    gc = _xdot_l(_as01(tri), g)
    gt = _xdot_l(_as01(same), g)
    half = 0.5 * gt
    qg = q * jnp.exp(gc - half)
    kg = (k * jnp.exp(half - gc)).astype(BF16)
    kd = k * jnp.exp(gt - gc)
    qs = q * jnp.exp(gc)
    vb = v.astype(BF16)

    head = _div(_iota((1, HW), 1), HD)
    o = jnp.zeros((BLK, HW), F32)
    for h in range(HEADS):
        hm = head == h
        att = _bdot_nt(jnp.where(hm, qg, 0.0), kg)
        att = jnp.where(tri, att, 0.0)
        o = o + jnp.where(hm, jnp.dot(att.astype(BF16), vb, preferred_element_type=F32), 0.0)
    return dict(o=o, qs=qs, v=v, kd=kd, gt=gt, ag=ag)


def _hgrn_prompt(lb_p, gn, e256, z3, layer, l_true):
    b, t, _ = z3.shape
    nr = NROW if b % NROW == 0 else 1
    kern = functools.partial(_hgrn_kernel, layer=layer, l_true=l_true)
    const = lambda shape: pl.BlockSpec(shape, lambda i, j: (0,) * len(shape))
    return pl.pallas_call(
        kern,
        grid=(b // nr, t // BLK),
        in_specs=[const(lb_p.shape), const((1, HW)), const((HW, HW)),
                  pl.BlockSpec((nr, BLK, 4 * HW), lambda i, j: (i, j, OFF_A // (4 * HW)))],
        out_specs=[pl.BlockSpec((nr, BLK, HW), lambda i, j: (i, j, 0)),
                   pl.BlockSpec((nr, HW, HW), lambda i, j: (i, 0, 0))],
        out_shape=[jax.ShapeDtypeStruct((b, t, HW), F32),
                   jax.ShapeDtypeStruct((b, HW, HW), F32)],
        scratch_shapes=[pltpu.VMEM((nr, HW, HW), F32)],
        compiler_params=_params("parallel", "arbitrary"),
        name="hgrn_prompt",
    )(lb_p, gn, e256, z3)


def _delta_kernel(cw_ref, alog_ref, dtb_ref, gn_ref, e_ref, zc_ref, zg_ref, zs_ref,
                  y_ref, st_ref, s_sc, xe_sc, *, l_true):
    tb = pl.program_id(1)

    @pl.when(tb == 0)
    def _():
        s_sc[...] = jnp.zeros_like(s_sc)
        xe_sc[:, 0:8, :] = jnp.zeros((xe_sc.shape[0], 8, C_CONV), F32)

    rows = range(zc_ref.shape[0])
    units = [(r, h) for r in rows for h in range(HEADS)]
    _, tri, strict = _chunk_masks(BLK)
    head = _div(_iota((1, HW), 1), HD)
    pre = [_delta_prep(r, tb, cw_ref, alog_ref, dtb_ref, e_ref, zc_ref, zs_ref, xe_sc, l_true) for r in rows]

    eye = jnp.where(_iota((BLK, BLK), 0) == _iota((BLK, BLK), 1), 1.0, 0.0)
    inv, pw, qks = {}, {}, {}
    for r, h in units:
        c = pre[r]
        d = c["gc"][:, h * HD:h * HD + 1] - c["gc_t"][h:h + 1, :]
        rel = jnp.where(tri, jnp.exp(jnp.where(tri, d, 0.0)), 0.0)
        kq = jnp.concatenate([jnp.where(head == h, c["k"], 0.0), jnp.where(head == h, c["q"], 0.0)], axis=0)
        kq = _bdot_nt(kq, c["kb"])
        a = jnp.where(strict, c["beta"][:, h * HD:h * HD + 1] * rel * kq[0:BLK], 0.0)
        qks[r, h] = rel * kq[BLK:2 * BLK]
        inv[r, h] = eye - a
        pw[r, h] = a
    levels = int(math.log2(CHUNK)) - 1
    for u in units:
        pw[u] = _bdot_rows([pw[u]], pw[u])[0]
    for lv in range(levels):
        for u in units:
            if lv + 1 < levels:
                step, pw[u] = _bdot_rows([inv[u], pw[u]], pw[u])
            else:
                step = _bdot_rows([inv[u]], pw[u])[0]
            inv[u] = inv[u] + step
    w_all = [jnp.zeros((BLK, HW), F32) for _ in rows]
    u_all = [jnp.zeros((BLK, HW), F32) for _ in rows]
    for r, h in units:
        sol = _bdot_rows([inv[r, h]], pre[r]["rhs"])[0]
        w_all[r] = w_all[r] + jnp.where(head == h, sol[:, 0:HW], 0.0)
        u_all[r] = u_all[r] + jnp.where(head == h, sol[:, HW:2 * HW], 0.0)

    diag = _div(_iota((HW, HW), 0), HD) == _div(_iota((HW, HW), 1), HD)
    outs = [[] for _ in rows]
    us = [[] for _ in rows]
    for j in range(BLK // CHUNK):
        sl = slice(j * CHUNK, (j + 1) * CHUNK)
        for r in rows:
            c = pre[r]
            st = s_sc[r]
            wq = jnp.concatenate([w_all[r][sl], c["qg"][sl]], axis=0)
            ws = jnp.dot(wq.astype(BF16), st.astype(BF16), preferred_element_type=F32)
            u = u_all[r][sl] - ws[0:CHUNK]
            outs[r].append(ws[CHUNK:2 * CHUNK])
            us[r].append(u)
            dec = jnp.exp(c["gt"][j * CHUNK:j * CHUNK + 1, :])
            s_sc[r] = st * dec + jnp.where(diag, _bdot_tn(c["kd"][sl], u), 0.0)

    for r in rows:
        o = jnp.concatenate(outs[r], axis=0)
        ub = jnp.concatenate(us[r], axis=0).astype(BF16)
        qu = jnp.dot(jnp.concatenate([qks[r, h].astype(BF16) for h in range(HEADS)], axis=0), ub,
                     preferred_element_type=F32)
        for h in range(HEADS):
            o = o + jnp.where(head == h, qu[h * BLK:(h + 1) * BLK], 0.0)
        ss = _xdot_r(o * o, e_ref[...], 2)
        y_ref[r] = o * lax.rsqrt(ss * (1.0 / HD) + EPS) * gn_ref[...] * _silu(zg_ref[r])

    @pl.when(tb == pl.num_programs(1) - 1)
    def _():
        st_ref[...] = s_sc[...]


def _delta_prep(r, tb, cw_ref, alog_ref, dtb_ref, e_ref, zc_ref, zs_ref, xe_sc, l_true):
    x = zc_ref[r]
    xe_sc[r, 8:8 + BLK, :] = x
    cw = cw_ref[...]
    conv = jnp.zeros((BLK, C_CONV), F32)
    for w in range(CONV_W):
        start = 8 - (CONV_W - 1) + w
        conv = conv + xe_sc[r, start:start + BLK, :] * cw[w:w + 1, :]
    xe_sc[r, 0:8, :] = x[BLK - 8:BLK, :]
    act = _silu(conv)
    qc, kc, v = act[:, 0:HW], act[:, HW:2 * HW], act[:, 2 * HW:3 * HW]
    e = e_ref[...]
    q = qc * lax.rsqrt(_xdot_r(qc * qc, e, 2) + EPS) * (HD ** -0.5)
    k = kc * lax.rsqrt(_xdot_r(kc * kc, e, 2) + EPS)

    zs = zs_ref[r]
    valid = (tb * BLK + _iota((BLK, 1), 0)) < l_true
    beta_s = jnp.where(valid, _sigmoid(zs), 0.0)
    la_s = jnp.where(valid, -jnp.exp(alog_ref[...]) * _softplus(zs + dtb_ref[...]), 0.0)
    src = _iota((128, HW), 0)
    dst_head = _div(_iota((128, HW), 1), HD)
    beta = _xdot_r(beta_s, _as01(src == dst_head))
    la = _xdot_r(la_s, _as01(src == dst_head + HEADS))

    same, tri, strict = _chunk_masks(BLK)
    gc = _xdot_l(_as01(tri), la)
    gt = _xdot_l(_as01(same), la)
    sel = _as01((_iota((8, HW), 1) == _iota((8, HW), 0) * HD) & (_iota((8, HW), 0) < HEADS))
    gc_t = _xdot_nt(sel, gc)
    eg = jnp.exp(gc)
    return dict(gc=gc, gt=gt, gc_t=gc_t, k=k, kb=k.astype(BF16), q=q, beta=beta,
                kd=k * jnp.exp(gt - gc), qg=q * eg,
                rhs=jnp.concatenate([beta * eg * k, beta * v], axis=1))


def _delta_prompt(cw, alog_p, dtb_p, gn, e256, z3, l_true):
    b, t, _ = z3.shape
    nr = NROW if b % NROW == 0 else 1
    kern = functools.partial(_delta_kernel, l_true=l_true)
    const = lambda shape: pl.BlockSpec(shape, lambda i, j: (0,) * len(shape))
    return pl.pallas_call(
        kern,
        grid=(b // nr, t // BLK),
        in_specs=[const((CONV_W, C_CONV)), const((1, 128)), const((1, 128)), const((1, HW)),
                  const((HW, HW)),
                  pl.BlockSpec((nr, BLK, C_CONV), lambda i, j: (i, j, OFF_CQKV // C_CONV)),
                  pl.BlockSpec((nr, BLK, HW), lambda i, j: (i, j, OFF_CG // HW)),
                  pl.BlockSpec((nr, BLK, 128), lambda i, j: (i, j, OFF_CS // 128))],
        out_specs=[pl.BlockSpec((nr, BLK, HW), lambda i, j: (i, j, 0)),
                   pl.BlockSpec((nr, HW, HW), lambda i, j: (i, 0, 0))],
        out_shape=[jax.ShapeDtypeStruct((b, t, HW), F32),
                   jax.ShapeDtypeStruct((b, HW, HW), F32)],
        scratch_shapes=[pltpu.VMEM((nr, HW, HW), F32), pltpu.VMEM((nr, 8 + BLK, C_CONV), F32)],
        compiler_params=_params("parallel", "arbitrary"),
        name="delta_prompt",
    )(cw, alog_p, dtb_p, gn, e256, z3, z3, z3)


def _sample_rec_kernel(lb_ref, gna_ref, gnc_ref, cw_ref, alog_ref, dtb_ref, z_ref, hst_ref, dst_ref,
                       cst_ref, ya_ref, yc_ref, hso_ref, dso_ref, *, layer):
    z = z_ref[0]
    eye = _iota((HW, HW), 0) == _iota((HW, HW), 1)

    def col(r):
        return jnp.sum(jnp.where(eye, r, 0.0), axis=1, keepdims=True)

    def head_norm(o, g, gate):
        return o * lax.rsqrt(jnp.mean(o * o, axis=-1, keepdims=True) + EPS) * g * _silu(gate)

    za = z[:, OFF_A:OFF_A + 4 * HW]
    aq, af = za[:, 0:HW], za[:, HW:2 * HW]
    v, ag = za[:, 2 * HW:3 * HW], za[:, 3 * HW:4 * HW]
    lb = _lower_bound(lb_ref[...], layer)
    f = lb + (1.0 - lb) * _sigmoid(af)
    fcol, kcol, qcol = col(f), col(1.0 - f), col(_silu(aq))
    for h in range(HEADS):
        sl = slice(h * HD, (h + 1) * HD)
        s = fcol[sl] * hst_ref[0, h] + kcol[sl] * v[:, sl]
        hso_ref[0, h] = s
        o = jnp.sum(qcol[sl] * s, axis=0, keepdims=True)
        ya_ref[0, :, sl] = head_norm(o, gna_ref[...], ag[:, sl])

    cst = cst_ref[0]
    cw = cw_ref[...]
    conv = z[:, OFF_CQKV:OFF_CQKV + C_CONV] * cw[CONV_W - 1:CONV_W]
    for w in range(CONV_W - 1):
        conv = conv + cst[w:w + 1] * cw[w:w + 1]
    act = _silu(conv)
    qc, kc, vc = act[:, 0:HW], act[:, HW:2 * HW], act[:, 2 * HW:3 * HW]
    cg = z[:, OFF_CG:OFF_CG + HW]
    zs = z[:, OFF_CS:OFF_CS + 128]
    beta_all = _sigmoid(zs)
    la_all = -jnp.exp(alog_ref[...]) * _softplus(zs + dtb_ref[...])
    qcol, kcol = col(qc), col(kc)
    for h in range(HEADS):
        sl = slice(h * HD, (h + 1) * HD)
        rq = lax.rsqrt(jnp.sum(qc[:, sl] * qc[:, sl], axis=-1, keepdims=True) + EPS) * (HD ** -0.5)
        rk = lax.rsqrt(jnp.sum(kc[:, sl] * kc[:, sl], axis=-1, keepdims=True) + EPS)
        kh, qh = kcol[sl] * rk, qcol[sl] * rq
        s = dst_ref[0, h] * jnp.exp(la_all[:, HEADS + h:HEADS + h + 1])
        u = beta_all[:, h:h + 1] * (vc[:, sl] - jnp.sum(kh * s, axis=0, keepdims=True))
        s = s + kh * u
        dso_ref[0, h] = s
        o = jnp.sum(qh * s, axis=0, keepdims=True)
        yc_ref[0, :, sl] = head_norm(o, gnc_ref[...], cg[:, sl])


def _sample_rec(lb_p, gna, gnc, cw, alog_p, dtb_p, zs3, hst, dst, cst, layer):
    nb = zs3.shape[0]
    kern = functools.partial(_sample_rec_kernel, layer=layer)
    const = lambda shape: pl.BlockSpec(shape, lambda i: (0,) * len(shape))
    st = pl.BlockSpec((1, HEADS, HD, HD), lambda i: (i, 0, 0, 0))
    yo = pl.BlockSpec((1, 1, HW), lambda i: (i, 0, 0))
    return pl.pallas_call(
        kern,
        grid=(nb,),
        in_specs=[const(lb_p.shape), const((1, HD)), const((1, HD)), const((CONV_W, C_CONV)),
                  const((1, 128)), const((1, 128)),
                  pl.BlockSpec((1, 1, NZ), lambda i: (i, 0, 0)), st, st,
                  pl.BlockSpec((1, CONV_W - 1, C_CONV), lambda i: (i, 0, 0))],
        out_specs=[yo, yo, st, st],
        out_shape=[jax.ShapeDtypeStruct((nb, 1, HW), F32)] * 2
                  + [jax.ShapeDtypeStruct((nb, HEADS, HD, HD), F32)] * 2,
        compiler_params=_params("parallel"),
        name="sample_rec",
    )(lb_p, gna, gnc, cw, alog_p, dtb_p, zs3, hst, dst, cst)


def _paged_kernel(pt_ref, lam_ref, gb_ref, q_ref, kn_ref, vn_ref, *rest, pps, lam_init, b_scale):
    k_refs, v_refs = rest[:pps], rest[pps:2 * pps]
    o_ref, m_sc, l_sc, acc_sc = rest[2 * pps:]
    s_id = pl.program_id(1)
    rows = k_refs[0].shape[0]
    rhead = _iota((8, 128), 0) & (HEADS - 1)
    lane_comp = _div(_iota((8, 128), 1), HD)

    def by_head(r):
        out = jnp.zeros((8, 128), F32)
        for h in range(HEADS):
            out = jnp.where(rhead == h, r[:, h * 128:(h + 1) * 128], out)
        return out

    def merge(x):
        return pltpu.roll(x, HEADS, 0)

    def widen(x, c):
        return jnp.where(lane_comp == c, x, pltpu.roll(x, HD, 1))

    @pl.when(s_id == 0)
    def _():
        m_sc[...] = jnp.full_like(m_sc, -jnp.inf)
        l_sc[...] = jnp.zeros_like(l_sc)
        acc_sc[...] = jnp.zeros_like(acc_sc)

    q8 = by_head(q_ref[0])
    comp_sum = _as01(_div(_iota((128, 128), 0), HD) == _div(_iota((128, 128), 1), HD))

    def scores(i):
        prod = (k_refs[i][...].reshape(rows // 8, 8, 128) * q8).reshape(rows, 128)
        return jnp.dot(prod.astype(BF16), comp_sum, preferred_element_type=F32).reshape(rows // 8, 8, 128)

    def swap(x):
        return pltpu.roll(x, HD, x.ndim - 1)

    def softmax(s3, m_old, l_old):
        mx = jnp.max(s3, axis=0)
        m_new = jnp.maximum(m_old, jnp.maximum(mx, merge(mx)))
        alpha = jnp.exp2(m_old - m_new)
        p = jnp.exp2(s3 - m_new)
        return m_new, alpha * l_old + jnp.sum(p, axis=0), alpha, p

    def weigh(i, alpha, p, acc):
        v3 = v_refs[i][...].reshape(rows // 8, 8, B_DV)
        p_sw = swap(p.reshape(rows, 128)).reshape(rows // 8, 8, 128)
        return [alpha * acc[0] + jnp.sum(p * v3, axis=0),
                swap(alpha) * acc[1] + jnp.sum(p_sw * v3, axis=0)]

    m, l, acc = m_sc[...], l_sc[...], [acc_sc[0], acc_sc[1]]
    s_q, w_q = {}, {}
    for t in range(pps + 2):
        if t < pps:
            s_q[t] = scores(t)
        if 1 <= t <= pps:
            m, l, alpha, pw = softmax(s_q.pop(t - 1), m, l)
            w_q[t - 1] = (alpha, pw)
        if t >= 2:
            acc = weigh(t - 2, *w_q.pop(t - 2), acc)
    m_sc[...] = m
    l_sc[...] = l
    acc_sc[0] = acc[0]
    acc_sc[1] = acc[1]

    @pl.when(s_id == pl.num_programs(1) - 1)
    def _():
        lam = _lambda(lam_ref[...], lam_init)
        qk_new = q8 * by_head(kn_ref[0])
        v_new = by_head(vn_ref[0])
        s_new = jnp.zeros((8, 128), F32)
        for c in range(2):
            s_new = jnp.where(lane_comp == c,
                              jnp.sum(jnp.where(lane_comp == c, qk_new, 0.0), axis=1, keepdims=True), s_new)
        m_old = m_sc[...]
        m_fin = jnp.maximum(m_old, s_new)
        alpha = jnp.exp2(m_old - m_fin)
        p_new = jnp.exp2(s_new - m_fin)
        l_fin = alpha * (l_sc[...] + merge(l_sc[...])) + p_new
        straight = acc_sc[0] + merge(acc_sc[0])
        crossed = acc_sc[1] + merge(acc_sc[1])
        nums = (jnp.where(lane_comp == 0, straight, crossed), jnp.where(lane_comp == 0, crossed, straight))
        outs = []
        for c in range(2):
            a_fin = widen(alpha, c) * nums[c] + widen(p_new, c) * v_new
            outs.append(a_fin / widen(l_fin, c))
        o = outs[0] - lam * outs[1]
        y = o * lax.rsqrt(jnp.mean(o * o, axis=-1, keepdims=True) + EPS) * gb_ref[...] * b_scale
        for h in range(HEADS):
            o_ref[0, :, h * B_DV:(h + 1) * B_DV] = y[h:h + 1]


def _paged_attn(page_table, lam_p, gb, qs3, kn3, vn3, cache_k, cache_v, layer, lam_init, b_scale, pps):
    nb, n_pages = page_table.shape
    depth, n_pool, page, nh, dd = cache_k.shape
    assert nh == HEADS and dd == 128 and cache_v.shape == cache_k.shape
    cache_k = cache_k.reshape(depth, n_pool, page * nh, dd)
    cache_v = cache_v.reshape(depth, n_pool, page * nh, dd)
    kern = functools.partial(_paged_kernel, pps=pps, lam_init=lam_init, b_scale=b_scale)
    row = pl.BlockSpec((1, 1, 2 * HW), lambda b, s, pt: (b, 0, 0))

    def page_spec(i):
        return pl.BlockSpec((None, None, page * nh, dd),
                            lambda b, s, pt: (layer, pt[b, s * pps + i], 0, 0))

    grid_spec = pltpu.PrefetchScalarGridSpec(
        num_scalar_prefetch=1,
        grid=(nb, n_pages // pps),
        in_specs=[pl.BlockSpec(lam_p.shape, lambda b, s, pt: (0, 0)),
                  pl.BlockSpec((1, B_DV), lambda b, s, pt: (0, 0)),
                  row, row, row]
                 + [page_spec(i) for i in range(pps)] * 2,
        out_specs=row,
        scratch_shapes=[pltpu.VMEM((8, 128), F32), pltpu.VMEM((8, 128), F32), pltpu.VMEM((2, 8, B_DV), F32)],
    )
    return pl.pallas_call(
        kern,
        grid_spec=grid_spec,
        out_shape=jax.ShapeDtypeStruct((nb, 1, 2 * HW), F32),
        compiler_params=_params("parallel", "arbitrary"),
        name="paged_attn",
    )(page_table, lam_p, gb, qs3, kn3, vn3, *([cache_k] * pps), *([cache_v] * pps))


def _diag_blocks(s, transpose):
    blocks = jnp.stack([s[:, h * HD:(h + 1) * HD, h * HD:(h + 1) * HD] for h in range(HEADS)], axis=1)
    return jnp.swapaxes(blocks, -1, -2) if transpose else blocks


def kernel(x_prompt, x_sample, cache_k, cache_v, page_table, state_hgrn, state_delta, state_conv,
           meta_tokens, norm_mix, norm_ffn, w_in, w_out, hgrn_lower_bound, hgrn_out_norm,
           diff_q_norm, diff_k_norm, diff_lambda, diff_out_norm, conv_w, delta_a_log,
           delta_dt_bias, delta_out_norm, w_up, w_down):
    nb_p, seq, d_model = x_prompt.shape
    nb_s = x_sample.shape[0]
    depth = w_in.shape[0]
    l_true = N_META + seq
    l_pad = -(-l_true // ABLK) * ABLK
    assert ABLK % BLK == 0 and x_sample.shape[1] == 1
    n_past = page_table.shape[1] * cache_k.shape[2]

    xp = jnp.concatenate([jnp.broadcast_to(meta_tokens[None].astype(F32), (nb_p, N_META, d_model)),
                          x_prompt, jnp.zeros((nb_p, l_pad - l_true, d_model), F32)], axis=1)
    xp = xp.reshape(nb_p * l_pad, d_model)
    xs = x_sample.reshape(nb_s, d_model)

    e256 = jnp.kron(jnp.eye(HEADS, dtype=F32), jnp.ones((HD, HD), F32)).astype(BF16)
    e512 = jnp.kron(jnp.eye(2 * HEADS, dtype=F32), jnp.ones((HD, HD), F32)).astype(BF16)
    tabs_p = _rope_tables(jnp.arange(l_pad))
    tabs_s = _rope_tables(jnp.full((nb_s,), n_past))
    w_in_p = jnp.concatenate([w_in[:, :, 2560:3584], w_in[:, :, 0:2560], w_in[:, :, 3584:D_IN],
                              jnp.zeros(w_in.shape[:2] + (NZ - D_IN,), w_in.dtype)], axis=2).astype(BF16)
    w_out_b, w_up_b, w_down_b = w_out.astype(BF16), w_up.astype(BF16), w_down.astype(BF16)
    lane_pad = lambda a: jnp.pad(a[None, :], ((0, 0), (HEADS, 128 - 2 * HEADS)))

    outs = {k: [] for k in ("kp", "vp", "ks", "vs", "hp", "hs", "dp", "ds", "cp", "cs")}
    for l in range(depth):
        lam_init = 0.8 - 0.6 * math.exp(-0.3 * l)
        b_scale = 1.0 - lam_init
        g_mix, g_ffn = norm_mix[l][None], norm_ffn[l][None]
        gq, gk = jnp.tile(diff_q_norm[l], 2 * HEADS)[None], jnp.tile(diff_k_norm[l], 2 * HEADS)[None]
        gna, gnc = hgrn_out_norm[l][None], delta_out_norm[l][None]
        gna_t, gnc_t = jnp.tile(gna, (1, HEADS)), jnp.tile(gnc, (1, HEADS))
        gb = diff_out_norm[l][None]
        alog_p, dtb_p = lane_pad(delta_a_log[l]), lane_pad(delta_dt_bias[l])

        z3 = _proj(xp, g_mix, w_in_p[l], _row_tile(xp.shape[0], 256)).reshape(nb_p, l_pad, NZ)
        ya, st_a = _hgrn_prompt(hgrn_lower_bound, gna_t, e256, z3, l, l_true)
        qn, kn = _qkrope(z3, gq, gk, e512, tabs_p, _row_tile(l_pad, 384))
        yb = _attn_prompt(diff_lambda[l], gb, qn, kn, z3, lam_init, b_scale)
        yc, st_c = _delta_prompt(conv_w[l], alog_p, dtb_p, gnc_t, e256, z3, l_true)
        rows = nb_p * l_pad
        xp = _merge_ffn(xp, ya.reshape(rows, -1), yb.reshape(rows, -1), yc.reshape(rows, -1),
                        w_out_b[l], g_ffn, w_up_b[l], w_down_b[l], _row_tile(rows, 512), 1024)
        outs["kp"].append(kn[:, :l_true].reshape(nb_p, l_true, HEADS, 2 * HD))
        outs["vp"].append(z3[:, :l_true, OFF_BV:OFF_BV + HEADS * B_DV].reshape(nb_p, l_true, HEADS, B_DV))
        outs["hp"].append(_diag_blocks(st_a, True))
        outs["dp"].append(_diag_blocks(st_c, False))
        outs["cp"].append(z3[:, l_true - (CONV_W - 1):l_true, OFF_CQKV:OFF_CQKV + C_CONV])

        zs = _proj(xs, g_mix, w_in_p[l], nb_s)
        zs3 = zs.reshape(nb_s, 1, NZ)
        ya_s, yc_s, hs_new, ds_new = _sample_rec(hgrn_lower_bound, gna, gnc, conv_w[l], alog_p, dtb_p,
                                                 zs3, state_hgrn[l], state_delta[l], state_conv[l], l)
        qn_s, kn_s = _qkrope(zs.reshape(1, nb_s, NZ), gq, gk, e512, tabs_s, nb_s)
        vn_s = zs[:, OFF_BV:OFF_BV + HEADS * B_DV]
        yb_s = _paged_attn(page_table, diff_lambda[l], gb, qn_s.reshape(nb_s, 1, -1),
                           kn_s.reshape(nb_s, 1, -1), vn_s.reshape(nb_s, 1, -1),
                           cache_k, cache_v, l, lam_init, b_scale, PAGES_PER_STEP)
        xs = _merge_ffn(xs, ya_s.reshape(nb_s, -1), yb_s.reshape(nb_s, -1), yc_s.reshape(nb_s, -1),
                        w_out_b[l], g_ffn, w_up_b[l], w_down_b[l], nb_s, 1024)
        outs["ks"].append(kn_s.reshape(nb_s, 1, HEADS, 2 * HD))
        outs["vs"].append(vn_s.reshape(nb_s, 1, HEADS, B_DV))
        outs["hs"].append(hs_new)
        outs["ds"].append(ds_new)
        outs["cs"].append(jnp.concatenate([state_conv[l][:, 1:], zs[:, None, OFF_CQKV:OFF_CQKV + C_CONV]],
                                          axis=1))

    y_prompt = xp.reshape(nb_p, l_pad, d_model)[:, N_META:l_true]
    y_sample = xs.reshape(nb_s, 1, d_model)
    return (y_prompt, y_sample) + tuple(jnp.stack(outs[k]) for k in
                                        ("kp", "vp", "ks", "vs", "hp", "hs", "dp", "ds", "cp", "cs"))
```

```python
import functools
import math

import jax
import jax.numpy as jnp
from jax import lax
from jax.experimental import pallas as pl
from jax.experimental.pallas import tpu as pltpu

F32 = jnp.float32
BF16 = jnp.bfloat16

EPS = 1e-6
N_META = 16
CHUNK = 16
BLK = 128
ABLK = 768
NROW = 4
PAGES_PER_STEP = 16
HEADS = 4
HD = 64
HW = HEADS * HD
B_DV = 128
ROT_DIM = 16
ROPE_THETA = 500000.0
Q_SCALE = HD ** -0.5 * math.log2(math.e)
CONV_W = 4
C_CONV = 3 * HW
D_IN = 3592
NZ = 3712
OFF_CQKV, OFF_CG = 0, 768
OFF_A = 1024
OFF_BQ, OFF_BK, OFF_BV = 2048, 2560, 3072
OFF_CS = 3584

VMEM_LIMIT = 48 * 1024 * 1024


def _bdot(a, b):
    return jnp.dot(a.astype(BF16), b.astype(BF16), preferred_element_type=F32)


def _bdot_nt(a, b):
    return lax.dot_general(a.astype(BF16), b.astype(BF16), (((1,), (1,)), ((), ())),
                           preferred_element_type=F32)


def _bdot_tn(a, b):
    return lax.dot_general(a.astype(BF16), b.astype(BF16), (((0,), (0,)), ((), ())),
                           preferred_element_type=F32)


def _split(a, n):
    parts, r = [], a
    for i in range(n):
        p = r.astype(BF16)
        parts.append(p)
        if i + 1 < n:
            r = r - p.astype(F32)
    return parts


def _xdot_l(m01, a, n=3):
    return sum(jnp.dot(m01, p, preferred_element_type=F32) for p in _split(a, n))


def _xdot_r(a, m01, n=3):
    return sum(jnp.dot(p, m01, preferred_element_type=F32) for p in _split(a, n))


def _xdot_nt(m01, a, n=3):
    return sum(lax.dot_general(m01, p, (((1,), (1,)), ((), ())), preferred_element_type=F32)
               for p in _split(a, n))


def _bdot_rows(lhs, b):
    m = lhs[0].shape[0]
    r = jnp.dot(jnp.concatenate([l.astype(BF16) for l in lhs], axis=0), b.astype(BF16),
                preferred_element_type=F32)
    return [r[i * m:(i + 1) * m] for i in range(len(lhs))]


def _sigmoid(x):
    return 1.0 / (1.0 + jnp.exp(-x))


def _silu(x):
    return x * _sigmoid(x)


def _softplus(x):
    return jnp.maximum(x, 0.0) + jnp.log(1.0 + jnp.exp(-jnp.abs(x)))


def _iota(shape, dim):
    return lax.broadcasted_iota(jnp.int32, shape, dim)


def _div(x, n):
    return lax.shift_right_logical(x, int(math.log2(n)))


def _chunk_masks(n):
    r, c = _iota((n, n), 0), _iota((n, n), 1)
    same = _div(r, CHUNK) == _div(c, CHUNK)
    return same, same & (c <= r), same & (c < r)


def _as01(mask):
    return jnp.where(mask, 1.0, 0.0).astype(BF16)


def _lower_bound(p, layer):
    mx = jnp.max(p, axis=0, keepdims=True)
    ex = jnp.exp(p - mx)
    soft = ex / jnp.sum(ex, axis=0, keepdims=True)
    if layer == 0:
        return jnp.zeros_like(soft[0:1])
    return jnp.sum(soft[1:layer + 1], axis=0, keepdims=True)


def _lambda(lv, lam_init):
    a = jnp.sum(lv[0:1] * lv[1:2], axis=1, keepdims=True)
    b = jnp.sum(lv[2:3] * lv[3:4], axis=1, keepdims=True)
    return jnp.exp(a) - jnp.exp(b) + lam_init


def _row_tile(m, want):
    t = min(want, m)
    while m % t:
        t //= 2
    assert t == m or t % 8 == 0, (m, want)
    return t


def _params(*sem):
    return pltpu.CompilerParams(dimension_semantics=sem, vmem_limit_bytes=VMEM_LIMIT)


def _proj_kernel(x_ref, g_ref, w_ref, o_ref):
    x = x_ref[...]
    h = x * lax.rsqrt(jnp.mean(x * x, axis=-1, keepdims=True) + EPS) * g_ref[...]
    o_ref[...] = jnp.dot(h.astype(BF16), w_ref[...], preferred_element_type=F32)


def _proj(x, g, w, tm):
    m, d = x.shape
    n = w.shape[1]
    return pl.pallas_call(
        _proj_kernel,
        grid=(m // tm,),
        in_specs=[pl.BlockSpec((tm, d), lambda i: (i, 0)),
                  pl.BlockSpec((1, d), lambda i: (0, 0)),
                  pl.BlockSpec((d, n), lambda i: (0, 0))],
        out_specs=pl.BlockSpec((tm, n), lambda i: (i, 0)),
        out_shape=jax.ShapeDtypeStruct((m, n), F32),
        compiler_params=_params("parallel"),
        name="proj",
    )(x, g, w)


def _ffn_kernel(x_ref, ya_ref, yb_ref, yc_ref, wo_ref, g_ref, wu_ref, wd_ref, o_ref, h_sc, acc_sc):
    k = pl.program_id(1)

    @pl.when(k == 0)
    def _():
        mix = jnp.concatenate([ya_ref[...], yb_ref[...], yc_ref[...]], axis=-1)
        x1 = x_ref[...] + jnp.dot(mix.astype(BF16), wo_ref[...], preferred_element_type=F32)
        h = x1 * lax.rsqrt(jnp.mean(x1 * x1, axis=-1, keepdims=True) + EPS) * g_ref[...]
        h_sc[...] = h.astype(BF16)
        acc_sc[...] = x1

    hid = jnp.dot(h_sc[...], wu_ref[...], preferred_element_type=F32)
    hid = jnp.square(jnp.maximum(hid, 0.0))
    acc_sc[...] += jnp.dot(hid.astype(BF16), wd_ref[...], preferred_element_type=F32)

    @pl.when(k == pl.num_programs(1) - 1)
    def _():
        o_ref[...] = acc_sc[...]


def _merge_ffn(x, ya, yb, yc, wo, g, wu, wd, tm, tf):
    m, d = x.shape
    dff = wu.shape[1]
    row = lambda i, k: (i, 0)
    return pl.pallas_call(
        _ffn_kernel,
        grid=(m // tm, dff // tf),
        in_specs=[pl.BlockSpec((tm, d), row),
                  pl.BlockSpec((tm, ya.shape[1]), row),
                  pl.BlockSpec((tm, yb.shape[1]), row),
                  pl.BlockSpec((tm, yc.shape[1]), row),
                  pl.BlockSpec(wo.shape, lambda i, k: (0, 0)),
                  pl.BlockSpec((1, d), lambda i, k: (0, 0)),
                  pl.BlockSpec((d, tf), lambda i, k: (0, k)),
                  pl.BlockSpec((tf, d), lambda i, k: (k, 0))],
        out_specs=pl.BlockSpec((tm, d), row),
        out_shape=jax.ShapeDtypeStruct((m, d), F32),
        scratch_shapes=[pltpu.VMEM((tm, d), BF16), pltpu.VMEM((tm, d), F32)],
        compiler_params=_params("parallel", "arbitrary"),
        name="merge_ffn",
    )(x, ya, yb, yc, wo, g, wu, wd)


def _qkrope_kernel(gq_ref, gk_ref, e_ref, c_ref, s1_ref, s2_ref, q_ref, k_ref, qo_ref, ko_ref):
    e = e_ref[...]
    cos, s_next, s_prev = c_ref[...], s1_ref[...], s2_ref[...]

    def norm_rope(x, g, scale):
        ss = _xdot_r(x * x, e, 2)
        xn = x * lax.rsqrt(ss * (1.0 / HD) + EPS) * g
        outs = []
        for h in range(HEADS):
            s = xn[:, h * 128:(h + 1) * 128]
            r = (s * cos + pltpu.roll(s, 128 - ROT_DIM // 2, 1) * s_next
                 + pltpu.roll(s, ROT_DIM // 2, 1) * s_prev)
            outs.append(r * scale if scale != 1.0 else r)
        return jnp.concatenate(outs, axis=1)

    qo_ref[0] = norm_rope(q_ref[0], gq_ref[...], Q_SCALE)
    ko_ref[0] = norm_rope(k_ref[0], gk_ref[...], 1.0)


def _qkrope(z3, gq, gk, e512, tabs, tt):
    b, t, _ = z3.shape
    w = 2 * HW
    const = lambda shape: pl.BlockSpec(shape, lambda i, j: (0,) * len(shape))
    tab = pl.BlockSpec((tt, 128), lambda i, j: (j, 0))
    out = pl.BlockSpec((1, tt, w), lambda i, j: (i, j, 0))
    return pl.pallas_call(
        _qkrope_kernel,
        grid=(b, t // tt),
        in_specs=[const((1, w)), const((1, w)), const((w, w)), tab, tab, tab,
                  pl.BlockSpec((1, tt, w), lambda i, j: (i, j, OFF_BQ // w)),
                  pl.BlockSpec((1, tt, w), lambda i, j: (i, j, OFF_BK // w))],
        out_specs=[out, out],
        out_shape=[jax.ShapeDtypeStruct((b, t, w), F32)] * 2,
        compiler_params=_params("parallel", "parallel"),
        name="qkrope",
    )(gq, gk, e512, *tabs, z3, z3)


def _rope_tables(pos):
    half = ROT_DIM // 2
    inv = jnp.power(ROPE_THETA, -jnp.arange(half, dtype=F32) * 2.0 / ROT_DIM)
    ang = pos.astype(F32)[:, None] * inv[None, :]
    cos, sin = jnp.cos(ang), jnp.sin(ang)
    n = pos.shape[0]
    one = jnp.ones((n, HD - ROT_DIM), F32)
    zero = jnp.zeros((n, HD - half), F32)
    c = jnp.concatenate([cos, cos, one], axis=1)
    s_next = jnp.concatenate([-sin, zero], axis=1)
    s_prev = jnp.concatenate([jnp.zeros((n, half), F32), sin, one * 0.0], axis=1)
    return tuple(jnp.tile(a, (1, 2)) for a in (c, s_next, s_prev))


def _attn_kernel(lam_ref, gb_ref, q_ref, k_ref, v_ref, o_ref, m_sc, a_sc, *, lam_init, b_scale):
    i = pl.program_id(2)
    q = q_ref[0]
    lane = _iota((1, 128), 1)
    qs = (jnp.where(lane < HD, q, 0.0).astype(BF16), jnp.where(lane >= HD, q, 0.0).astype(BF16))
    m_sc[...] = jnp.full_like(m_sc, -jnp.inf)
    a_sc[...] = jnp.zeros_like(a_sc)
    ones = jnp.ones((ABLK, 128), BF16)

    def block(j, keep):
        start = pl.multiple_of(j * ABLK, ABLK)
        kb = k_ref[0, pl.ds(start, ABLK), :].astype(BF16)
        vb = jnp.concatenate([v_ref[0, pl.ds(start, ABLK), :].astype(BF16), ones], axis=1)
        scores = [lax.dot_general(qs[c], kb, (((1,), (1,)), ((), ())), preferred_element_type=F32)
                  for c in range(2)]
        for c in range(2):
            s = scores[c] if keep is None else jnp.where(keep, scores[c], -jnp.inf)
            m_old = m_sc[c]
            m_new = jnp.maximum(m_old, jnp.max(s, axis=1, keepdims=True))
            alpha = jnp.exp2(m_old - m_new)
            p = jnp.concatenate([jnp.exp2(s[:, n * 128:(n + 1) * 128] - m_new) for n in range(ABLK // 128)],
                                axis=1)
            a_sc[c] = (jnp.concatenate([alpha, alpha], axis=1) * a_sc[c]
                       + jnp.dot(p.astype(BF16), vb, preferred_element_type=F32))
            m_sc[c] = m_new

    def body(j, carry):
        block(j, None)
        return carry

    lax.fori_loop(0, i, body, 0)
    block(i, _iota((ABLK, ABLK), 1) <= _iota((ABLK, ABLK), 0))

    lam = _lambda(lam_ref[...], lam_init)
    o = (a_sc[0, :, 0:B_DV] / a_sc[0, :, B_DV:2 * B_DV]
         - lam * (a_sc[1, :, 0:B_DV] / a_sc[1, :, B_DV:2 * B_DV]))
    y = o * lax.rsqrt(jnp.mean(o * o, axis=-1, keepdims=True) + EPS) * gb_ref[...]
    o_ref[0] = y * b_scale


def _attn_prompt(lam_p, gb, qn, kn, z3, lam_init, b_scale):
    b, t, _ = qn.shape
    kern = functools.partial(_attn_kernel, lam_init=lam_init, b_scale=b_scale)
    return pl.pallas_call(
        kern,
        grid=(b, HEADS, t // ABLK),
        in_specs=[pl.BlockSpec(lam_p.shape, lambda i, h, j: (0, 0)),
                  pl.BlockSpec((1, B_DV), lambda i, h, j: (0, 0)),
                  pl.BlockSpec((1, ABLK, 128), lambda i, h, j: (i, j, h)),
                  pl.BlockSpec((1, t, 128), lambda i, h, j: (i, 0, h)),
                  pl.BlockSpec((1, t, B_DV), lambda i, h, j: (i, 0, OFF_BV // B_DV + h))],
        out_specs=pl.BlockSpec((1, ABLK, B_DV), lambda i, h, j: (i, j, h)),
        out_shape=jax.ShapeDtypeStruct((b, t, HEADS * B_DV), F32),
        scratch_shapes=[pltpu.VMEM((2, ABLK, 128), F32), pltpu.VMEM((2, ABLK, 2 * B_DV), F32)],
        compiler_params=_params("parallel", "parallel", "arbitrary"),
        name="attn_prompt",
    )(lam_p, gb, qn, kn, z3)


def _hgrn_kernel(lb_ref, gn_ref, e_ref, z_ref, y_ref, st_ref, s_sc, *, layer, l_true):
    tb = pl.program_id(1)

    @pl.when(tb == 0)
    def _():
        s_sc[...] = jnp.zeros_like(s_sc)

    rows = range(z_ref.shape[0])
    pre = [_hgrn_prep(r, tb, lb_ref, z_ref, layer, l_true) for r in rows]

    diag = _div(_iota((HW, HW), 0), HD) == _div(_iota((HW, HW), 1), HD)
    outs = [[] for _ in rows]
    for j in range(BLK // CHUNK):
        sl = slice(j * CHUNK, (j + 1) * CHUNK)
        for r in rows:
            c = pre[r]
            st = s_sc[r]
            outs[r].append(_bdot_nt(c["qs"][sl], st))
            dec = jnp.exp(c["gt"][j * CHUNK:j * CHUNK + 1, :])
            s_sc[r] = st * dec + jnp.where(diag, _bdot_tn(c["v"][sl], c["kd"][sl]), 0.0)

    for r in rows:
        o = pre[r]["o"] + jnp.concatenate(outs[r], axis=0)
        ss = _xdot_r(o * o, e_ref[...], 2)
        y_ref[r] = o * lax.rsqrt(ss * (1.0 / HD) + EPS) * gn_ref[...] * _silu(pre[r]["ag"])

    @pl.when(tb == pl.num_programs(1) - 1)
    def _():
        st_ref[...] = s_sc[...]


def _hgrn_prep(r, tb, lb_ref, z_ref, layer, l_true):
    z = z_ref[r]
    aq, af = z[:, 0:HW], z[:, HW:2 * HW]
    v, ag = z[:, 2 * HW:3 * HW], z[:, 3 * HW:4 * HW]
    lb = _lower_bound(lb_ref[...], layer)
    f = lb + (1.0 - lb) * _sigmoid(af)
    valid = (tb * BLK + _iota((BLK, 1), 0)) < l_true
    k = jnp.where(valid, 1.0 - f, 0.0)
    g = jnp.where(valid, jnp.log(f), 0.0)
    q = _silu(aq)

    same, tri, _ = _chunk_masks(BLK)
    gc = _xdot_l(_as01(tri), g)
    gt = _xdot_l(_as01(same), g)
    half = 0.5 * gt
    qg = q * jnp.exp(gc - half)
    kg = (k * jnp.exp(half - gc)).astype(BF16)
    kd = k * jnp.exp(gt - gc)
    qs = q * jnp.exp(gc)
    vb = v.astype(BF16)

    head = _div(_iota((1, HW), 1), HD)
    o = jnp.zeros((BLK, HW), F32)
    for h in range(HEADS):
        hm = head == h
        att = _bdot_nt(jnp.where(hm, qg, 0.0), kg)
        att = jnp.where(tri, att, 0.0)
        o = o + jnp.where(hm, jnp.dot(att.astype(BF16), vb, preferred_element_type=F32), 0.0)
    return dict(o=o, qs=qs, v=v, kd=kd, gt=gt, ag=ag)


def _hgrn_prompt(lb_p, gn, e256, z3, layer, l_true):
    b, t, _ = z3.shape
    nr = NROW if b % NROW == 0 else 1
    kern = functools.partial(_hgrn_kernel, layer=layer, l_true=l_true)
    const = lambda shape: pl.BlockSpec(shape, lambda i, j: (0,) * len(shape))
    return pl.pallas_call(
        kern,
        grid=(b // nr, t // BLK),
        in_specs=[const(lb_p.shape), const((1, HW)), const((HW, HW)),
                  pl.BlockSpec((nr, BLK, 4 * HW), lambda i, j: (i, j, OFF_A // (4 * HW)))],
        out_specs=[pl.BlockSpec((nr, BLK, HW), lambda i, j: (i, j, 0)),
                   pl.BlockSpec((nr, HW, HW), lambda i, j: (i, 0, 0))],
        out_shape=[jax.ShapeDtypeStruct((b, t, HW), F32),
                   jax.ShapeDtypeStruct((b, HW, HW), F32)],
        scratch_shapes=[pltpu.VMEM((nr, HW, HW), F32)],
        compiler_params=_params("parallel", "arbitrary"),
        name="hgrn_prompt",
    )(lb_p, gn, e256, z3)


def _delta_kernel(cw_ref, alog_ref, dtb_ref, gn_ref, e_ref, zc_ref, zg_ref, zs_ref,
                  y_ref, st_ref, s_sc, xe_sc, *, l_true):
    tb = pl.program_id(1)

    @pl.when(tb == 0)
    def _():
        s_sc[...] = jnp.zeros_like(s_sc)
        xe_sc[:, 0:8, :] = jnp.zeros((xe_sc.shape[0], 8, C_CONV), F32)

    rows = range(zc_ref.shape[0])
    units = [(r, h) for r in rows for h in range(HEADS)]
    _, tri, strict = _chunk_masks(BLK)
    head = _div(_iota((1, HW), 1), HD)
    pre = [_delta_prep(r, tb, cw_ref, alog_ref, dtb_ref, e_ref, zc_ref, zs_ref, xe_sc, l_true) for r in rows]

    eye = jnp.where(_iota((BLK, BLK), 0) == _iota((BLK, BLK), 1), 1.0, 0.0)
    inv, pw, qks = {}, {}, {}
    for r, h in units:
        c = pre[r]
        d = c["gc"][:, h * HD:h * HD + 1] - c["gc_t"][h:h + 1, :]
        rel = jnp.where(tri, jnp.exp(jnp.where(tri, d, 0.0)), 0.0)
        kq = jnp.concatenate([jnp.where(head == h, c["k"], 0.0), jnp.where(head == h, c["q"], 0.0)], axis=0)
        kq = _bdot_nt(kq, c["kb"])
        a = jnp.where(strict, c["beta"][:, h * HD:h * HD + 1] * rel * kq[0:BLK], 0.0)
        qks[r, h] = rel * kq[BLK:2 * BLK]
        inv[r, h] = eye - a
        pw[r, h] = a
    levels = int(math.log2(CHUNK)) - 1
    for u in units:
        pw[u] = _bdot_rows([pw[u]], pw[u])[0]
    for lv in range(levels):
        for u in units:
            if lv + 1 < levels:
                step, pw[u] = _bdot_rows([inv[u], pw[u]], pw[u])
            else:
                step = _bdot_rows([inv[u]], pw[u])[0]
            inv[u] = inv[u] + step
    w_all = [jnp.zeros((BLK, HW), F32) for _ in rows]
    u_all = [jnp.zeros((BLK, HW), F32) for _ in rows]
    for r, h in units:
        sol = _bdot_rows([inv[r, h]], pre[r]["rhs"])[0]
        w_all[r] = w_all[r] + jnp.where(head == h, sol[:, 0:HW], 0.0)
        u_all[r] = u_all[r] + jnp.where(head == h, sol[:, HW:2 * HW], 0.0)

    diag = _div(_iota((HW, HW), 0), HD) == _div(_iota((HW, HW), 1), HD)
    outs = [[] for _ in rows]
    us = [[] for _ in rows]
    for j in range(BLK // CHUNK):
        sl = slice(j * CHUNK, (j + 1) * CHUNK)
        for r in rows:
            c = pre[r]
            st = s_sc[r]
            wq = jnp.concatenate([w_all[r][sl], c["qg"][sl]], axis=0)
            ws = jnp.dot(wq.astype(BF16), st.astype(BF16), preferred_element_type=F32)
            u = u_all[r][sl] - ws[0:CHUNK]
            outs[r].append(ws[CHUNK:2 * CHUNK])
            us[r].append(u)
            dec = jnp.exp(c["gt"][j * CHUNK:j * CHUNK + 1, :])
            s_sc[r] = st * dec + jnp.where(diag, _bdot_tn(c["kd"][sl], u), 0.0)

    for r in rows:
        o = jnp.concatenate(outs[r], axis=0)
        ub = jnp.concatenate(us[r], axis=0).astype(BF16)
        qu = jnp.dot(jnp.concatenate([qks[r, h].astype(BF16) for h in range(HEADS)], axis=0), ub,
                     preferred_element_type=F32)
        for h in range(HEADS):
            o = o + jnp.where(head == h, qu[h * BLK:(h + 1) * BLK], 0.0)
        ss = _xdot_r(o * o, e_ref[...], 2)
        y_ref[r] = o * lax.rsqrt(ss * (1.0 / HD) + EPS) * gn_ref[...] * _silu(zg_ref[r])

    @pl.when(tb == pl.num_programs(1) - 1)
    def _():
        st_ref[...] = s_sc[...]


def _delta_prep(r, tb, cw_ref, alog_ref, dtb_ref, e_ref, zc_ref, zs_ref, xe_sc, l_true):
    x = zc_ref[r]
    xe_sc[r, 8:8 + BLK, :] = x
    cw = cw_ref[...]
    conv = jnp.zeros((BLK, C_CONV), F32)
    for w in range(CONV_W):
        start = 8 - (CONV_W - 1) + w
        conv = conv + xe_sc[r, start:start + BLK, :] * cw[w:w + 1, :]
    xe_sc[r, 0:8, :] = x[BLK - 8:BLK, :]
    act = _silu(conv)
    qc, kc, v = act[:, 0:HW], act[:, HW:2 * HW], act[:, 2 * HW:3 * HW]
    e = e_ref[...]
    q = qc * lax.rsqrt(_xdot_r(qc * qc, e, 2) + EPS) * (HD ** -0.5)
    k = kc * lax.rsqrt(_xdot_r(kc * kc, e, 2) + EPS)

    zs = zs_ref[r]
    valid = (tb * BLK + _iota((BLK, 1), 0)) < l_true
    beta_s = jnp.where(valid, _sigmoid(zs), 0.0)
    la_s = jnp.where(valid, -jnp.exp(alog_ref[...]) * _softplus(zs + dtb_ref[...]), 0.0)
    src = _iota((128, HW), 0)
    dst_head = _div(_iota((128, HW), 1), HD)
    beta = _xdot_r(beta_s, _as01(src == dst_head))
    la = _xdot_r(la_s, _as01(src == dst_head + HEADS))

    same, tri, strict = _chunk_masks(BLK)
    gc = _xdot_l(_as01(tri), la)
    gt = _xdot_l(_as01(same), la)
    sel = _as01((_iota((8, HW), 1) == _iota((8, HW), 0) * HD) & (_iota((8, HW), 0) < HEADS))
    gc_t = _xdot_nt(sel, gc)
    eg = jnp.exp(gc)
    return dict(gc=gc, gt=gt, gc_t=gc_t, k=k, kb=k.astype(BF16), q=q, beta=beta,
                kd=k * jnp.exp(gt - gc), qg=q * eg,
                rhs=jnp.concatenate([beta * eg * k, beta * v], axis=1))


def _delta_prompt(cw, alog_p, dtb_p, gn, e256, z3, l_true):
    b, t, _ = z3.shape
    nr = NROW if b % NROW == 0 else 1
    kern = functools.partial(_delta_kernel, l_true=l_true)
    const = lambda shape: pl.BlockSpec(shape, lambda i, j: (0,) * len(shape))
    return pl.pallas_call(
        kern,
        grid=(b // nr, t // BLK),
        in_specs=[const((CONV_W, C_CONV)), const((1, 128)), const((1, 128)), const((1, HW)),
                  const((HW, HW)),
                  pl.BlockSpec((nr, BLK, C_CONV), lambda i, j: (i, j, OFF_CQKV // C_CONV)),
                  pl.BlockSpec((nr, BLK, HW), lambda i, j: (i, j, OFF_CG // HW)),
                  pl.BlockSpec((nr, BLK, 128), lambda i, j: (i, j, OFF_CS // 128))],
        out_specs=[pl.BlockSpec((nr, BLK, HW), lambda i, j: (i, j, 0)),
                   pl.BlockSpec((nr, HW, HW), lambda i, j: (i, 0, 0))],
        out_shape=[jax.ShapeDtypeStruct((b, t, HW), F32),
                   jax.ShapeDtypeStruct((b, HW, HW), F32)],
        scratch_shapes=[pltpu.VMEM((nr, HW, HW), F32), pltpu.VMEM((nr, 8 + BLK, C_CONV), F32)],
        compiler_params=_params("parallel", "arbitrary"),
        name="delta_prompt",
    )(cw, alog_p, dtb_p, gn, e256, z3, z3, z3)


def _sample_rec_kernel(lb_ref, gna_ref, gnc_ref, cw_ref, alog_ref, dtb_ref, z_ref, hst_ref, dst_ref,
                       cst_ref, ya_ref, yc_ref, hso_ref, dso_ref, *, layer):
    z = z_ref[0]
    eye = _iota((HW, HW), 0) == _iota((HW, HW), 1)

    def col(r):
        return jnp.sum(jnp.where(eye, r, 0.0), axis=1, keepdims=True)

    def head_norm(o, g, gate):
        return o * lax.rsqrt(jnp.mean(o * o, axis=-1, keepdims=True) + EPS) * g * _silu(gate)

    za = z[:, OFF_A:OFF_A + 4 * HW]
    aq, af = za[:, 0:HW], za[:, HW:2 * HW]
    v, ag = za[:, 2 * HW:3 * HW], za[:, 3 * HW:4 * HW]
    lb = _lower_bound(lb_ref[...], layer)
    f = lb + (1.0 - lb) * _sigmoid(af)
    fcol, kcol, qcol = col(f), col(1.0 - f), col(_silu(aq))
    for h in range(HEADS):
        sl = slice(h * HD, (h + 1) * HD)
        s = fcol[sl] * hst_ref[0, h] + kcol[sl] * v[:, sl]
        hso_ref[0, h] = s
        o = jnp.sum(qcol[sl] * s, axis=0, keepdims=True)
        ya_ref[0, :, sl] = head_norm(o, gna_ref[...], ag[:, sl])

    cst = cst_ref[0]
    cw = cw_ref[...]
    conv = z[:, OFF_CQKV:OFF_CQKV + C_CONV] * cw[CONV_W - 1:CONV_W]
    for w in range(CONV_W - 1):
        conv = conv + cst[w:w + 1] * cw[w:w + 1]
    act = _silu(conv)
    qc, kc, vc = act[:, 0:HW], act[:, HW:2 * HW], act[:, 2 * HW:3 * HW]
    cg = z[:, OFF_CG:OFF_CG + HW]
    zs = z[:, OFF_CS:OFF_CS + 128]
    beta_all = _sigmoid(zs)
    la_all = -jnp.exp(alog_ref[...]) * _softplus(zs + dtb_ref[...])
    qcol, kcol = col(qc), col(kc)
    for h in range(HEADS):
        sl = slice(h * HD, (h + 1) * HD)
        rq = lax.rsqrt(jnp.sum(qc[:, sl] * qc[:, sl], axis=-1, keepdims=True) + EPS) * (HD ** -0.5)
        rk = lax.rsqrt(jnp.sum(kc[:, sl] * kc[:, sl], axis=-1, keepdims=True) + EPS)
        kh, qh = kcol[sl] * rk, qcol[sl] * rq
        s = dst_ref[0, h] * jnp.exp(la_all[:, HEADS + h:HEADS + h + 1])
        u = beta_all[:, h:h + 1] * (vc[:, sl] - jnp.sum(kh * s, axis=0, keepdims=True))
        s = s + kh * u
        dso_ref[0, h] = s
        o = jnp.sum(qh * s, axis=0, keepdims=True)
        yc_ref[0, :, sl] = head_norm(o, gnc_ref[...], cg[:, sl])


def _sample_rec(lb_p, gna, gnc, cw, alog_p, dtb_p, zs3, hst, dst, cst, layer):
    nb = zs3.shape[0]
    kern = functools.partial(_sample_rec_kernel, layer=layer)
    const = lambda shape: pl.BlockSpec(shape, lambda i: (0,) * len(shape))
    st = pl.BlockSpec((1, HEADS, HD, HD), lambda i: (i, 0, 0, 0))
    yo = pl.BlockSpec((1, 1, HW), lambda i: (i, 0, 0))
    return pl.pallas_call(
        kern,
        grid=(nb,),
        in_specs=[const(lb_p.shape), const((1, HD)), const((1, HD)), const((CONV_W, C_CONV)),
                  const((1, 128)), const((1, 128)),
                  pl.BlockSpec((1, 1, NZ), lambda i: (i, 0, 0)), st, st,
                  pl.BlockSpec((1, CONV_W - 1, C_CONV), lambda i: (i, 0, 0))],
        out_specs=[yo, yo, st, st],
        out_shape=[jax.ShapeDtypeStruct((nb, 1, HW), F32)] * 2
                  + [jax.ShapeDtypeStruct((nb, HEADS, HD, HD), F32)] * 2,
        compiler_params=_params("parallel"),
        name="sample_rec",
    )(lb_p, gna, gnc, cw, alog_p, dtb_p, zs3, hst, dst, cst)


def _paged_kernel(pt_ref, lam_ref, gb_ref, q_ref, kn_ref, vn_ref, *rest, pps, lam_init, b_scale):
    k_refs, v_refs = rest[:pps], rest[pps:2 * pps]
    o_ref, m_sc, l_sc, acc_sc = rest[2 * pps:]
    s_id = pl.program_id(1)
    rows = k_refs[0].shape[0]
    rhead = _iota((8, 128), 0) & (HEADS - 1)
    lane_comp = _div(_iota((8, 128), 1), HD)

    def by_head(r):
        out = jnp.zeros((8, 128), F32)
        for h in range(HEADS):
            out = jnp.where(rhead == h, r[:, h * 128:(h + 1) * 128], out)
        return out

    def merge(x):
        return pltpu.roll(x, HEADS, 0)

    def widen(x, c):
        return jnp.where(lane_comp == c, x, pltpu.roll(x, HD, 1))

    @pl.when(s_id == 0)
    def _():
        m_sc[...] = jnp.full_like(m_sc, -jnp.inf)
        l_sc[...] = jnp.zeros_like(l_sc)
        acc_sc[...] = jnp.zeros_like(acc_sc)

    q8 = by_head(q_ref[0])
    comp_sum = _as01(_div(_iota((128, 128), 0), HD) == _div(_iota((128, 128), 1), HD))

    def scores(i):
        prod = (k_refs[i][...].reshape(rows // 8, 8, 128) * q8).reshape(rows, 128)
        return jnp.dot(prod.astype(BF16), comp_sum, preferred_element_type=F32).reshape(rows // 8, 8, 128)

    def swap(x):
        return pltpu.roll(x, HD, x.ndim - 1)

    def softmax(s3, m_old, l_old):
        mx = jnp.max(s3, axis=0)
        m_new = jnp.maximum(m_old, jnp.maximum(mx, merge(mx)))
        alpha = jnp.exp2(m_old - m_new)
        p = jnp.exp2(s3 - m_new)
        return m_new, alpha * l_old + jnp.sum(p, axis=0), alpha, p

    def weigh(i, alpha, p, acc):
        v3 = v_refs[i][...].reshape(rows // 8, 8, B_DV)
        p_sw = swap(p.reshape(rows, 128)).reshape(rows // 8, 8, 128)
        return [alpha * acc[0] + jnp.sum(p * v3, axis=0),
                swap(alpha) * acc[1] + jnp.sum(p_sw * v3, axis=0)]

    m, l, acc = m_sc[...], l_sc[...], [acc_sc[0], acc_sc[1]]
    s_q, w_q = {}, {}
    for t in range(pps + 2):
        if t < pps:
            s_q[t] = scores(t)
        if 1 <= t <= pps:
            m, l, alpha, pw = softmax(s_q.pop(t - 1), m, l)
            w_q[t - 1] = (alpha, pw)
        if t >= 2:
            acc = weigh(t - 2, *w_q.pop(t - 2), acc)
    m_sc[...] = m
    l_sc[...] = l
    acc_sc[0] = acc[0]
    acc_sc[1] = acc[1]

    @pl.when(s_id == pl.num_programs(1) - 1)
    def _():
        lam = _lambda(lam_ref[...], lam_init)
        qk_new = q8 * by_head(kn_ref[0])
        v_new = by_head(vn_ref[0])
        s_new = jnp.zeros((8, 128), F32)
        for c in range(2):
            s_new = jnp.where(lane_comp == c,
                              jnp.sum(jnp.where(lane_comp == c, qk_new, 0.0), axis=1, keepdims=True), s_new)
        m_old = m_sc[...]
        m_fin = jnp.maximum(m_old, s_new)
        alpha = jnp.exp2(m_old - m_fin)
        p_new = jnp.exp2(s_new - m_fin)
        l_fin = alpha * (l_sc[...] + merge(l_sc[...])) + p_new
        straight = acc_sc[0] + merge(acc_sc[0])
        crossed = acc_sc[1] + merge(acc_sc[1])
        nums = (jnp.where(lane_comp == 0, straight, crossed), jnp.where(lane_comp == 0, crossed, straight))
        outs = []
        for c in range(2):
            a_fin = widen(alpha, c) * nums[c] + widen(p_new, c) * v_new
            outs.append(a_fin / widen(l_fin, c))
        o = outs[0] - lam * outs[1]
        y = o * lax.rsqrt(jnp.mean(o * o, axis=-1, keepdims=True) + EPS) * gb_ref[...] * b_scale
        for h in range(HEADS):
            o_ref[0, :, h * B_DV:(h + 1) * B_DV] = y[h:h + 1]


def _paged_attn(page_table, lam_p, gb, qs3, kn3, vn3, cache_k, cache_v, layer, lam_init, b_scale, pps):
    nb, n_pages = page_table.shape
    depth, n_pool, page, nh, dd = cache_k.shape
    assert nh == HEADS and dd == 128 and cache_v.shape == cache_k.shape
    cache_k = cache_k.reshape(depth, n_pool, page * nh, dd)
    cache_v = cache_v.reshape(depth, n_pool, page * nh, dd)
    kern = functools.partial(_paged_kernel, pps=pps, lam_init=lam_init, b_scale=b_scale)
    row = pl.BlockSpec((1, 1, 2 * HW), lambda b, s, pt: (b, 0, 0))

    def page_spec(i):
        return pl.BlockSpec((None, None, page * nh, dd),
                            lambda b, s, pt: (layer, pt[b, s * pps + i], 0, 0))

    grid_spec = pltpu.PrefetchScalarGridSpec(
        num_scalar_prefetch=1,
        grid=(nb, n_pages // pps),
        in_specs=[pl.BlockSpec(lam_p.shape, lambda b, s, pt: (0, 0)),
                  pl.BlockSpec((1, B_DV), lambda b, s, pt: (0, 0)),
                  row, row, row]
                 + [page_spec(i) for i in range(pps)] * 2,
        out_specs=row,
        scratch_shapes=[pltpu.VMEM((8, 128), F32), pltpu.VMEM((8, 128), F32), pltpu.VMEM((2, 8, B_DV), F32)],
    )
    return pl.pallas_call(
        kern,
        grid_spec=grid_spec,
        out_shape=jax.ShapeDtypeStruct((nb, 1, 2 * HW), F32),
        compiler_params=_params("parallel", "arbitrary"),
        name="paged_attn",
    )(page_table, lam_p, gb, qs3, kn3, vn3, *([cache_k] * pps), *([cache_v] * pps))


def _diag_blocks(s, transpose):
    blocks = jnp.stack([s[:, h * HD:(h + 1) * HD, h * HD:(h + 1) * HD] for h in range(HEADS)], axis=1)
    return jnp.swapaxes(blocks, -1, -2) if transpose else blocks


def kernel(x_prompt, x_sample, cache_k, cache_v, page_table, state_hgrn, state_delta, state_conv,
           meta_tokens, norm_mix, norm_ffn, w_in, w_out, hgrn_lower_bound, hgrn_out_norm,
           diff_q_norm, diff_k_norm, diff_lambda, diff_out_norm, conv_w, delta_a_log,
           delta_dt_bias, delta_out_norm, w_up, w_down):
    nb_p, seq, d_model = x_prompt.shape
    nb_s = x_sample.shape[0]
    depth = w_in.shape[0]
    l_true = N_META + seq
    l_pad = -(-l_true // ABLK) * ABLK
    assert ABLK % BLK == 0 and x_sample.shape[1] == 1
    n_past = page_table.shape[1] * cache_k.shape[2]

    xp = jnp.concatenate([jnp.broadcast_to(meta_tokens[None].astype(F32), (nb_p, N_META, d_model)),
                          x_prompt, jnp.zeros((nb_p, l_pad - l_true, d_model), F32)], axis=1)
    xp = xp.reshape(nb_p * l_pad, d_model)
    xs = x_sample.reshape(nb_s, d_model)

    e256 = jnp.kron(jnp.eye(HEADS, dtype=F32), jnp.ones((HD, HD), F32)).astype(BF16)
    e512 = jnp.kron(jnp.eye(2 * HEADS, dtype=F32), jnp.ones((HD, HD), F32)).astype(BF16)
    tabs_p = _rope_tables(jnp.arange(l_pad))
    tabs_s = _rope_tables(jnp.full((nb_s,), n_past))
    w_in_p = jnp.concatenate([w_in[:, :, 2560:3584], w_in[:, :, 0:2560], w_in[:, :, 3584:D_IN],
                              jnp.zeros(w_in.shape[:2] + (NZ - D_IN,), w_in.dtype)], axis=2).astype(BF16)
    w_out_b, w_up_b, w_down_b = w_out.astype(BF16), w_up.astype(BF16), w_down.astype(BF16)
    lane_pad = lambda a: jnp.pad(a[None, :], ((0, 0), (HEADS, 128 - 2 * HEADS)))

    outs = {k: [] for k in ("kp", "vp", "ks", "vs", "hp", "hs", "dp", "ds", "cp", "cs")}
    for l in range(depth):
        lam_init = 0.8 - 0.6 * math.exp(-0.3 * l)
        b_scale = 1.0 - lam_init
        g_mix, g_ffn = norm_mix[l][None], norm_ffn[l][None]
        gq, gk = jnp.tile(diff_q_norm[l], 2 * HEADS)[None], jnp.tile(diff_k_norm[l], 2 * HEADS)[None]
        gna, gnc = hgrn_out_norm[l][None], delta_out_norm[l][None]
        gna_t, gnc_t = jnp.tile(gna, (1, HEADS)), jnp.tile(gnc, (1, HEADS))
        gb = diff_out_norm[l][None]
        alog_p, dtb_p = lane_pad(delta_a_log[l]), lane_pad(delta_dt_bias[l])

        z3 = _proj(xp, g_mix, w_in_p[l], _row_tile(xp.shape[0], 256)).reshape(nb_p, l_pad, NZ)
        ya, st_a = _hgrn_prompt(hgrn_lower_bound, gna_t, e256, z3, l, l_true)
        qn, kn = _qkrope(z3, gq, gk, e512, tabs_p, _row_tile(l_pad, 384))
        yb = _attn_prompt(diff_lambda[l], gb, qn, kn, z3, lam_init, b_scale)
        yc, st_c = _delta_prompt(conv_w[l], alog_p, dtb_p, gnc_t, e256, z3, l_true)
        rows = nb_p * l_pad
        xp = _merge_ffn(xp, ya.reshape(rows, -1), yb.reshape(rows, -1), yc.reshape(rows, -1),
                        w_out_b[l], g_ffn, w_up_b[l], w_down_b[l], _row_tile(rows, 512), 1024)
        outs["kp"].append(kn[:, :l_true].reshape(nb_p, l_true, HEADS, 2 * HD))
        outs["vp"].append(z3[:, :l_true, OFF_BV:OFF_BV + HEADS * B_DV].reshape(nb_p, l_true, HEADS, B_DV))
        outs["hp"].append(_diag_blocks(st_a, True))
        outs["dp"].append(_diag_blocks(st_c, False))
        outs["cp"].append(z3[:, l_true - (CONV_W - 1):l_true, OFF_CQKV:OFF_CQKV + C_CONV])

        zs = _proj(xs, g_mix, w_in_p[l], nb_s)
        zs3 = zs.reshape(nb_s, 1, NZ)
        ya_s, yc_s, hs_new, ds_new = _sample_rec(hgrn_lower_bound, gna, gnc, conv_w[l], alog_p, dtb_p,
                                                 zs3, state_hgrn[l], state_delta[l], state_conv[l], l)
        qn_s, kn_s = _qkrope(zs.reshape(1, nb_s, NZ), gq, gk, e512, tabs_s, nb_s)
        vn_s = zs[:, OFF_BV:OFF_BV + HEADS * B_DV]
        yb_s = _paged_attn(page_table, diff_lambda[l], gb, qn_s.reshape(nb_s, 1, -1),
                           kn_s.reshape(nb_s, 1, -1), vn_s.reshape(nb_s, 1, -1),
                           cache_k, cache_v, l, lam_init, b_scale, PAGES_PER_STEP)
        xs = _merge_ffn(xs, ya_s.reshape(nb_s, -1), yb_s.reshape(nb_s, -1), yc_s.reshape(nb_s, -1),
                        w_out_b[l], g_ffn, w_up_b[l], w_down_b[l], nb_s, 1024)
        outs["ks"].append(kn_s.reshape(nb_s, 1, HEADS, 2 * HD))
        outs["vs"].append(vn_s.reshape(nb_s, 1, HEADS, B_DV))
        outs["hs"].append(hs_new)
        outs["ds"].append(ds_new)
        outs["cs"].append(jnp.concatenate([state_conv[l][:, 1:], zs[:, None, OFF_CQKV:OFF_CQKV + C_CONV]],
                                          axis=1))

    y_prompt = xp.reshape(nb_p, l_pad, d_model)[:, N_META:l_true]
    y_sample = xs.reshape(nb_s, 1, d_model)
    return (y_prompt, y_sample) + tuple(jnp.stack(outs[k]) for k in
                                        ("kp", "vp", "ks", "vs", "hp", "hs", "dp", "ds", "cp", "cs"))
```

```python
import functools
import math

import jax
import jax.numpy as jnp
from jax import lax
from jax.experimental import pallas as pl
from jax.experimental.pallas import tpu as pltpu

F32 = jnp.float32
BF16 = jnp.bfloat16

EPS = 1e-6
N_META = 16
CHUNK = 16
BLK = 128
ABLK = 768
NROW = 8
PAGES_PER_STEP = 16
HEADS = 4
HD = 64
HW = HEADS * HD
B_DV = 128
ROT_DIM = 16
ROPE_THETA = 500000.0
Q_SCALE = HD ** -0.5 * math.log2(math.e)
CONV_W = 4
C_CONV = 3 * HW
D_IN = 3592
NZ = 3712
OFF_CQKV, OFF_CG = 0, 768
OFF_A = 1024
OFF_BQ, OFF_BK, OFF_BV = 2048, 2560, 3072
OFF_CS = 3584

VMEM_LIMIT = 48 * 1024 * 1024


def _bdot(a, b):
    return jnp.dot(a.astype(BF16), b.astype(BF16), preferred_element_type=F32)


def _bdot_nt(a, b):
    return lax.dot_general(a.astype(BF16), b.astype(BF16), (((1,), (1,)), ((), ())),
                           preferred_element_type=F32)


def _bdot_tn(a, b):
    return lax.dot_general(a.astype(BF16), b.astype(BF16), (((0,), (0,)), ((), ())),
                           preferred_element_type=F32)


def _split(a, n):
    parts, r = [], a
    for i in range(n):
        p = r.astype(BF16)
        parts.append(p)
        if i + 1 < n:
            r = r - p.astype(F32)
    return parts


def _xdot_l(m01, a, n=3):
    return sum(jnp.dot(m01, p, preferred_element_type=F32) for p in _split(a, n))


def _xdot_r(a, m01, n=3):
    return sum(jnp.dot(p, m01, preferred_element_type=F32) for p in _split(a, n))


def _xdot_nt(m01, a, n=3):
    return sum(lax.dot_general(m01, p, (((1,), (1,)), ((), ())), preferred_element_type=F32)
               for p in _split(a, n))


def _bdot_rows(lhs, b):
    m = lhs[0].shape[0]
    r = jnp.dot(jnp.concatenate([l.astype(BF16) for l in lhs], axis=0), b.astype(BF16),
                preferred_element_type=F32)
    return [r[i * m:(i + 1) * m] for i in range(len(lhs))]


def _sigmoid(x):
    return 1.0 / (1.0 + jnp.exp(-x))


def _silu(x):
    return x * _sigmoid(x)


def _softplus(x):
    return jnp.maximum(x, 0.0) + jnp.log(1.0 + jnp.exp(-jnp.abs(x)))


def _iota(shape, dim):
    return lax.broadcasted_iota(jnp.int32, shape, dim)


def _div(x, n):
    return lax.shift_right_logical(x, int(math.log2(n)))


def _chunk_masks(n):
    r, c = _iota((n, n), 0), _iota((n, n), 1)
    same = _div(r, CHUNK) == _div(c, CHUNK)
    return same, same & (c <= r), same & (c < r)


def _as01(mask):
    return jnp.where(mask, 1.0, 0.0).astype(BF16)


def _lower_bound(p, layer):
    mx = jnp.max(p, axis=0, keepdims=True)
    ex = jnp.exp(p - mx)
    soft = ex / jnp.sum(ex, axis=0, keepdims=True)
    if layer == 0:
        return jnp.zeros_like(soft[0:1])
    return jnp.sum(soft[1:layer + 1], axis=0, keepdims=True)


def _lambda(lv, lam_init):
    a = jnp.sum(lv[0:1] * lv[1:2], axis=1, keepdims=True)
    b = jnp.sum(lv[2:3] * lv[3:4], axis=1, keepdims=True)
    return jnp.exp(a) - jnp.exp(b) + lam_init


def _row_tile(m, want):
    t = min(want, m)
    while m % t:
        t //= 2
    assert t == m or t % 8 == 0, (m, want)
    return t


def _params(*sem):
    return pltpu.CompilerParams(dimension_semantics=sem, vmem_limit_bytes=VMEM_LIMIT)


def _proj_kernel(x_ref, g_ref, w_ref, o_ref):
    x = x_ref[...]
    h = x * lax.rsqrt(jnp.mean(x * x, axis=-1, keepdims=True) + EPS) * g_ref[...]
    o_ref[...] = jnp.dot(h.astype(BF16), w_ref[...], preferred_element_type=F32)


def _proj(x, g, w, tm):
    m, d = x.shape
    n = w.shape[1]
    return pl.pallas_call(
        _proj_kernel,
        grid=(m // tm,),
        in_specs=[pl.BlockSpec((tm, d), lambda i: (i, 0)),
                  pl.BlockSpec((1, d), lambda i: (0, 0)),
                  pl.BlockSpec((d, n), lambda i: (0, 0))],
        out_specs=pl.BlockSpec((tm, n), lambda i: (i, 0)),
        out_shape=jax.ShapeDtypeStruct((m, n), F32),
        compiler_params=_params("parallel"),
        name="proj",
    )(x, g, w)


def _ffn_kernel(x_ref, ya_ref, yb_ref, yc_ref, wo_ref, g_ref, wu_ref, wd_ref, o_ref, h_sc, acc_sc):
    k = pl.program_id(1)

    @pl.when(k == 0)
    def _():
        mix = jnp.concatenate([ya_ref[...], yb_ref[...], yc_ref[...]], axis=-1)
        x1 = x_ref[...] + jnp.dot(mix.astype(BF16), wo_ref[...], preferred_element_type=F32)
        h = x1 * lax.rsqrt(jnp.mean(x1 * x1, axis=-1, keepdims=True) + EPS) * g_ref[...]
        h_sc[...] = h.astype(BF16)
        acc_sc[...] = x1

    hid = jnp.dot(h_sc[...], wu_ref[...], preferred_element_type=F32)
    hid = jnp.square(jnp.maximum(hid, 0.0))
    acc_sc[...] += jnp.dot(hid.astype(BF16), wd_ref[...], preferred_element_type=F32)

    @pl.when(k == pl.num_programs(1) - 1)
    def _():
        o_ref[...] = acc_sc[...]


def _merge_ffn(x, ya, yb, yc, wo, g, wu, wd, tm, tf):
    m, d = x.shape
    dff = wu.shape[1]
    row = lambda i, k: (i, 0)
    return pl.pallas_call(
        _ffn_kernel,
        grid=(m // tm, dff // tf),
        in_specs=[pl.BlockSpec((tm, d), row),
                  pl.BlockSpec((tm, ya.shape[1]), row),
                  pl.BlockSpec((tm, yb.shape[1]), row),
                  pl.BlockSpec((tm, yc.shape[1]), row),
                  pl.BlockSpec(wo.shape, lambda i, k: (0, 0)),
                  pl.BlockSpec((1, d), lambda i, k: (0, 0)),
                  pl.BlockSpec((d, tf), lambda i, k: (0, k)),
                  pl.BlockSpec((tf, d), lambda i, k: (k, 0))],
        out_specs=pl.BlockSpec((tm, d), row),
        out_shape=jax.ShapeDtypeStruct((m, d), F32),
        scratch_shapes=[pltpu.VMEM((tm, d), BF16), pltpu.VMEM((tm, d), F32)],
        compiler_params=_params("parallel", "arbitrary"),
        name="merge_ffn",
    )(x, ya, yb, yc, wo, g, wu, wd)


def _qkrope_kernel(gq_ref, gk_ref, e_ref, c_ref, s1_ref, s2_ref, q_ref, k_ref, qo_ref, ko_ref):
    e = e_ref[...]
    cos, s_next, s_prev = c_ref[...], s1_ref[...], s2_ref[...]

    def norm_rope(x, g, scale):
        ss = _xdot_r(x * x, e, 2)
        xn = x * lax.rsqrt(ss * (1.0 / HD) + EPS) * g
        outs = []
        for h in range(HEADS):
            s = xn[:, h * 128:(h + 1) * 128]
            r = (s * cos + pltpu.roll(s, 128 - ROT_DIM // 2, 1) * s_next
                 + pltpu.roll(s, ROT_DIM // 2, 1) * s_prev)
            outs.append(r * scale if scale != 1.0 else r)
        return jnp.concatenate(outs, axis=1)

    qo_ref[0] = norm_rope(q_ref[0], gq_ref[...], Q_SCALE)
    ko_ref[0] = norm_rope(k_ref[0], gk_ref[...], 1.0)


def _qkrope(z3, gq, gk, e512, tabs, tt):
    b, t, _ = z3.shape
    w = 2 * HW
    const = lambda shape: pl.BlockSpec(shape, lambda i, j: (0,) * len(shape))
    tab = pl.BlockSpec((tt, 128), lambda i, j: (j, 0))
    out = pl.BlockSpec((1, tt, w), lambda i, j: (i, j, 0))
    return pl.pallas_call(
        _qkrope_kernel,
        grid=(b, t // tt),
        in_specs=[const((1, w)), const((1, w)), const((w, w)), tab, tab, tab,
                  pl.BlockSpec((1, tt, w), lambda i, j: (i, j, OFF_BQ // w)),
                  pl.BlockSpec((1, tt, w), lambda i, j: (i, j, OFF_BK // w))],
        out_specs=[out, out],
        out_shape=[jax.ShapeDtypeStruct((b, t, w), F32)] * 2,
        compiler_params=_params("parallel", "parallel"),
        name="qkrope",
    )(gq, gk, e512, *tabs, z3, z3)


def _rope_tables(pos):
    half = ROT_DIM // 2
    inv = jnp.power(ROPE_THETA, -jnp.arange(half, dtype=F32) * 2.0 / ROT_DIM)
    ang = pos.astype(F32)[:, None] * inv[None, :]
    cos, sin = jnp.cos(ang), jnp.sin(ang)
    n = pos.shape[0]
    one = jnp.ones((n, HD - ROT_DIM), F32)
    zero = jnp.zeros((n, HD - half), F32)
    c = jnp.concatenate([cos, cos, one], axis=1)
    s_next = jnp.concatenate([-sin, zero], axis=1)
    s_prev = jnp.concatenate([jnp.zeros((n, half), F32), sin, one * 0.0], axis=1)
    return tuple(jnp.tile(a, (1, 2)) for a in (c, s_next, s_prev))


def _attn_kernel(lam_ref, gb_ref, q_ref, k_ref, v_ref, o_ref, m_sc, a_sc, *, lam_init, b_scale):
    i = pl.program_id(2)
    q = q_ref[0]
    lane = _iota((1, 128), 1)
    qs = (jnp.where(lane < HD, q, 0.0).astype(BF16), jnp.where(lane >= HD, q, 0.0).astype(BF16))
    m_sc[...] = jnp.full_like(m_sc, -jnp.inf)
    a_sc[...] = jnp.zeros_like(a_sc)
    ones = jnp.ones((ABLK, 128), BF16)

    def block(j, keep):
        start = pl.multiple_of(j * ABLK, ABLK)
        kb = k_ref[0, pl.ds(start, ABLK), :].astype(BF16)
        vb = jnp.concatenate([v_ref[0, pl.ds(start, ABLK), :].astype(BF16), ones], axis=1)
        scores = [lax.dot_general(qs[c], kb, (((1,), (1,)), ((), ())), preferred_element_type=F32)
                  for c in range(2)]
        for c in range(2):
            s = scores[c] if keep is None else jnp.where(keep, scores[c], -jnp.inf)
            m_old = m_sc[c]
            m_new = jnp.maximum(m_old, jnp.max(s, axis=1, keepdims=True))
            alpha = jnp.exp2(m_old - m_new)
            p = jnp.concatenate([jnp.exp2(s[:, n * 128:(n + 1) * 128] - m_new) for n in range(ABLK // 128)],
                                axis=1)
            a_sc[c] = (jnp.concatenate([alpha, alpha], axis=1) * a_sc[c]
                       + jnp.dot(p.astype(BF16), vb, preferred_element_type=F32))
            m_sc[c] = m_new

    def body(j, carry):
        block(j, None)
        return carry

    lax.fori_loop(0, i, body, 0)
    block(i, _iota((ABLK, ABLK), 1) <= _iota((ABLK, ABLK), 0))

    lam = _lambda(lam_ref[...], lam_init)
    o = (a_sc[0, :, 0:B_DV] / a_sc[0, :, B_DV:2 * B_DV]
         - lam * (a_sc[1, :, 0:B_DV] / a_sc[1, :, B_DV:2 * B_DV]))
    y = o * lax.rsqrt(jnp.mean(o * o, axis=-1, keepdims=True) + EPS) * gb_ref[...]
    o_ref[0] = y * b_scale


def _attn_prompt(lam_p, gb, qn, kn, z3, lam_init, b_scale):
    b, t, _ = qn.shape
    kern = functools.partial(_attn_kernel, lam_init=lam_init, b_scale=b_scale)
    return pl.pallas_call(
        kern,
        grid=(b, HEADS, t // ABLK),
        in_specs=[pl.BlockSpec(lam_p.shape, lambda i, h, j: (0, 0)),
                  pl.BlockSpec((1, B_DV), lambda i, h, j: (0, 0)),
                  pl.BlockSpec((1, ABLK, 128), lambda i, h, j: (i, j, h)),
                  pl.BlockSpec((1, t, 128), lambda i, h, j: (i, 0, h)),
                  pl.BlockSpec((1, t, B_DV), lambda i, h, j: (i, 0, OFF_BV // B_DV + h))],
        out_specs=pl.BlockSpec((1, ABLK, B_DV), lambda i, h, j: (i, j, h)),
        out_shape=jax.ShapeDtypeStruct((b, t, HEADS * B_DV), F32),
        scratch_shapes=[pltpu.VMEM((2, ABLK, 128), F32), pltpu.VMEM((2, ABLK, 2 * B_DV), F32)],
        compiler_params=_params("parallel", "parallel", "arbitrary"),
        name="attn_prompt",
    )(lam_p, gb, qn, kn, z3)


def _hgrn_kernel(lb_ref, gn_ref, e_ref, z_ref, y_ref, st_ref, s_sc, *, layer, l_true):
    tb = pl.program_id(1)

    @pl.when(tb == 0)
    def _():
        s_sc[...] = jnp.zeros_like(s_sc)

    rows = range(z_ref.shape[0])
    pre = [_hgrn_prep(r, tb, lb_ref, z_ref, layer, l_true) for r in rows]

    diag = _div(_iota((HW, HW), 0), HD) == _div(_iota((HW, HW), 1), HD)
    outs = [[] for _ in rows]
    for j in range(BLK // CHUNK):
        sl = slice(j * CHUNK, (j + 1) * CHUNK)
        for r in rows:
            c = pre[r]
            st = s_sc[r]
            outs[r].append(_bdot_nt(c["qs"][sl], st))
            dec = jnp.exp(c["gt"][j * CHUNK:j * CHUNK + 1, :])
            s_sc[r] = st * dec + jnp.where(diag, _bdot_tn(c["v"][sl], c["kd"][sl]), 0.0)

    for r in rows:
        o = pre[r]["o"] + jnp.concatenate(outs[r], axis=0)
        ss = _xdot_r(o * o, e_ref[...], 2)
        y_ref[r] = o * lax.rsqrt(ss * (1.0 / HD) + EPS) * gn_ref[...] * _silu(pre[r]["ag"])

    @pl.when(tb == pl.num_programs(1) - 1)
    def _():
        st_ref[...] = s_sc[...]


def _hgrn_prep(r, tb, lb_ref, z_ref, layer, l_true):
    z = z_ref[r]
    aq, af = z[:, 0:HW], z[:, HW:2 * HW]
    v, ag = z[:, 2 * HW:3 * HW], z[:, 3 * HW:4 * HW]
    lb = _lower_bound(lb_ref[...], layer)
    f = lb + (1.0 - lb) * _sigmoid(af)
    valid = (tb * BLK + _iota((BLK, 1), 0)) < l_true
    k = jnp.where(valid, 1.0 - f, 0.0)
    g = jnp.where(valid, jnp.log(f), 0.0)
    q = _silu(aq)

    same, tri, _ = _chunk_masks(BLK)
    gc = _xdot_l(_as01(tri), g)
    gt = _xdot_l(_as01(same), g)
    half = 0.5 * gt
    qg = q * jnp.exp(gc - half)
    kg = (k * jnp.exp(half - gc)).astype(BF16)
    kd = k * jnp.exp(gt - gc)
    qs = q * jnp.exp(gc)
    vb = v.astype(BF16)

    head = _div(_iota((1, HW), 1), HD)
    o = jnp.zeros((BLK, HW), F32)
    for h in range(HEADS):
        hm = head == h
        att = _bdot_nt(jnp.where(hm, qg, 0.0), kg)
        att = jnp.where(tri, att, 0.0)
        o = o + jnp.where(hm, jnp.dot(att.astype(BF16), vb, preferred_element_type=F32), 0.0)
    return dict(o=o, qs=qs, v=v, kd=kd, gt=gt, ag=ag)


def _hgrn_prompt(lb_p, gn, e256, z3, layer, l_true):
    b, t, _ = z3.shape
    nr = NROW if b % NROW == 0 else 1
    kern = functools.partial(_hgrn_kernel, layer=layer, l_true=l_true)
    const = lambda shape: pl.BlockSpec(shape, lambda i, j: (0,) * len(shape))
    return pl.pallas_call(
        kern,
        grid=(b // nr, t // BLK),
        in_specs=[const(lb_p.shape), const((1, HW)), const((HW, HW)),
                  pl.BlockSpec((nr, BLK, 4 * HW), lambda i, j: (i, j, OFF_A // (4 * HW)))],
        out_specs=[pl.BlockSpec((nr, BLK, HW), lambda i, j: (i, j, 0)),
                   pl.BlockSpec((nr, HW, HW), lambda i, j: (i, 0, 0))],
        out_shape=[jax.ShapeDtypeStruct((b, t, HW), F32),
                   jax.ShapeDtypeStruct((b, HW, HW), F32)],
        scratch_shapes=[pltpu.VMEM((nr, HW, HW), F32)],
        compiler_params=_params("parallel", "arbitrary"),
        name="hgrn_prompt",
    )(lb_p, gn, e256, z3)


def _delta_kernel(cw_ref, alog_ref, dtb_ref, gn_ref, e_ref, zc_ref, zg_ref, zs_ref,
                  y_ref, st_ref, s_sc, xe_sc, *, l_true):
    tb = pl.program_id(1)

    @pl.when(tb == 0)
    def _():
        s_sc[...] = jnp.zeros_like(s_sc)
        xe_sc[:, 0:8, :] = jnp.zeros((xe_sc.shape[0], 8, C_CONV), F32)

    rows = range(zc_ref.shape[0])
    units = [(r, h) for r in rows for h in range(HEADS)]
    _, tri, strict = _chunk_masks(BLK)
    head = _div(_iota((1, HW), 1), HD)
    pre = [_delta_prep(r, tb, cw_ref, alog_ref, dtb_ref, e_ref, zc_ref, zs_ref, xe_sc, l_true) for r in rows]

    eye = jnp.where(_iota((BLK, BLK), 0) == _iota((BLK, BLK), 1), 1.0, 0.0)
    inv, pw, qks = {}, {}, {}
    for r, h in units:
        c = pre[r]
        d = c["gc"][:, h * HD:h * HD + 1] - c["gc_t"][h:h + 1, :]
        rel = jnp.where(tri, jnp.exp(jnp.where(tri, d, 0.0)), 0.0)
        kq = jnp.concatenate([jnp.where(head == h, c["k"], 0.0), jnp.where(head == h, c["q"], 0.0)], axis=0)
        kq = _bdot_nt(kq, c["kb"])
        a = jnp.where(strict, c["beta"][:, h * HD:h * HD + 1] * rel * kq[0:BLK], 0.0)
        qks[r, h] = rel * kq[BLK:2 * BLK]
        inv[r, h] = eye - a
        pw[r, h] = a
    levels = int(math.log2(CHUNK)) - 1
    for u in units:
        pw[u] = _bdot_rows([pw[u]], pw[u])[0]
    for lv in range(levels):
        for u in units:
            if lv + 1 < levels:
                step, pw[u] = _bdot_rows([inv[u], pw[u]], pw[u])
            else:
                step = _bdot_rows([inv[u]], pw[u])[0]
            inv[u] = inv[u] + step
    w_all = [jnp.zeros((BLK, HW), F32) for _ in rows]
    u_all = [jnp.zeros((BLK, HW), F32) for _ in rows]
    for r, h in units:
        sol = _bdot_rows([inv[r, h]], pre[r]["rhs"])[0]
        w_all[r] = w_all[r] + jnp.where(head == h, sol[:, 0:HW], 0.0)
        u_all[r] = u_all[r] + jnp.where(head == h, sol[:, HW:2 * HW], 0.0)

    diag = _div(_iota((HW, HW), 0), HD) == _div(_iota((HW, HW), 1), HD)
    outs = [[] for _ in rows]
    us = [[] for _ in rows]
    for j in range(BLK // CHUNK):
        sl = slice(j * CHUNK, (j + 1) * CHUNK)
        for r in rows:
            c = pre[r]
            st = s_sc[r]
            wq = jnp.concatenate([w_all[r][sl], c["qg"][sl]], axis=0)
            ws = jnp.dot(wq.astype(BF16), st.astype(BF16), preferred_element_type=F32)
            u = u_all[r][sl] - ws[0:CHUNK]
            outs[r].append(ws[CHUNK:2 * CHUNK])
            us[r].append(u)
            dec = jnp.exp(c["gt"][j * CHUNK:j * CHUNK + 1, :])
            s_sc[r] = st * dec + jnp.where(diag, _bdot_tn(c["kd"][sl], u), 0.0)

    for r in rows:
        o = jnp.concatenate(outs[r], axis=0)
        ub = jnp.concatenate(us[r], axis=0).astype(BF16)
        qu = jnp.dot(jnp.concatenate([qks[r, h].astype(BF16) for h in range(HEADS)], axis=0), ub,
                     preferred_element_type=F32)
        for h in range(HEADS):
            o = o + jnp.where(head == h, qu[h * BLK:(h + 1) * BLK], 0.0)
        ss = _xdot_r(o * o, e_ref[...], 2)
        y_ref[r] = o * lax.rsqrt(ss * (1.0 / HD) + EPS) * gn_ref[...] * _silu(zg_ref[r])

    @pl.when(tb == pl.num_programs(1) - 1)
    def _():
        st_ref[...] = s_sc[...]


def _delta_prep(r, tb, cw_ref, alog_ref, dtb_ref, e_ref, zc_ref, zs_ref, xe_sc, l_true):
    x = zc_ref[r]
    xe_sc[r, 8:8 + BLK, :] = x
    cw = cw_ref[...]
    conv = jnp.zeros((BLK, C_CONV), F32)
    for w in range(CONV_W):
        start = 8 - (CONV_W - 1) + w
        conv = conv + xe_sc[r, start:start + BLK, :] * cw[w:w + 1, :]
    xe_sc[r, 0:8, :] = x[BLK - 8:BLK, :]
    act = _silu(conv)
    qc, kc, v = act[:, 0:HW], act[:, HW:2 * HW], act[:, 2 * HW:3 * HW]
    e = e_ref[...]
    q = qc * lax.rsqrt(_xdot_r(qc * qc, e, 2) + EPS) * (HD ** -0.5)
    k = kc * lax.rsqrt(_xdot_r(kc * kc, e, 2) + EPS)

    zs = zs_ref[r]
    valid = (tb * BLK + _iota((BLK, 1), 0)) < l_true
    beta_s = jnp.where(valid, _sigmoid(zs), 0.0)
    la_s = jnp.where(valid, -jnp.exp(alog_ref[...]) * _softplus(zs + dtb_ref[...]), 0.0)
    src = _iota((128, HW), 0)
    dst_head = _div(_iota((128, HW), 1), HD)
    beta = _xdot_r(beta_s, _as01(src == dst_head))
    la = _xdot_r(la_s, _as01(src == dst_head + HEADS))

    same, tri, strict = _chunk_masks(BLK)
    gc = _xdot_l(_as01(tri), la)
    gt = _xdot_l(_as01(same), la)
    sel = _as01((_iota((8, HW), 1) == _iota((8, HW), 0) * HD) & (_iota((8, HW), 0) < HEADS))
    gc_t = _xdot_nt(sel, gc)
    eg = jnp.exp(gc)
    return dict(gc=gc, gt=gt, gc_t=gc_t, k=k, kb=k.astype(BF16), q=q, beta=beta,
                kd=k * jnp.exp(gt - gc), qg=q * eg,
                rhs=jnp.concatenate([beta * eg * k, beta * v], axis=1))


def _delta_prompt(cw, alog_p, dtb_p, gn, e256, z3, l_true):
    b, t, _ = z3.shape
    nr = NROW if b % NROW == 0 else 1
    kern = functools.partial(_delta_kernel, l_true=l_true)
    const = lambda shape: pl.BlockSpec(shape, lambda i, j: (0,) * len(shape))
    return pl.pallas_call(
        kern,
        grid=(b // nr, t // BLK),
        in_specs=[const((CONV_W, C_CONV)), const((1, 128)), const((1, 128)), const((1, HW)),
                  const((HW, HW)),
                  pl.BlockSpec((nr, BLK, C_CONV), lambda i, j: (i, j, OFF_CQKV // C_CONV)),
                  pl.BlockSpec((nr, BLK, HW), lambda i, j: (i, j, OFF_CG // HW)),
                  pl.BlockSpec((nr, BLK, 128), lambda i, j: (i, j, OFF_CS // 128))],
        out_specs=[pl.BlockSpec((nr, BLK, HW), lambda i, j: (i, j, 0)),
                   pl.BlockSpec((nr, HW, HW), lambda i, j: (i, 0, 0))],
        out_shape=[jax.ShapeDtypeStruct((b, t, HW), F32),
                   jax.ShapeDtypeStruct((b, HW, HW), F32)],
        scratch_shapes=[pltpu.VMEM((nr, HW, HW), F32), pltpu.VMEM((nr, 8 + BLK, C_CONV), F32)],
        compiler_params=_params("parallel", "arbitrary"),
        name="delta_prompt",
    )(cw, alog_p, dtb_p, gn, e256, z3, z3, z3)


def _sample_rec_kernel(lb_ref, gna_ref, gnc_ref, cw_ref, alog_ref, dtb_ref, z_ref, hst_ref, dst_ref,
                       cst_ref, ya_ref, yc_ref, hso_ref, dso_ref, *, layer):
    z = z_ref[0]
    eye = _iota((HW, HW), 0) == _iota((HW, HW), 1)

    def col(r):
        return jnp.sum(jnp.where(eye, r, 0.0), axis=1, keepdims=True)

    def head_norm(o, g, gate):
        return o * lax.rsqrt(jnp.mean(o * o, axis=-1, keepdims=True) + EPS) * g * _silu(gate)

    za = z[:, OFF_A:OFF_A + 4 * HW]
    aq, af = za[:, 0:HW], za[:, HW:2 * HW]
    v, ag = za[:, 2 * HW:3 * HW], za[:, 3 * HW:4 * HW]
    lb = _lower_bound(lb_ref[...], layer)
    f = lb + (1.0 - lb) * _sigmoid(af)
    fcol, kcol, qcol = col(f), col(1.0 - f), col(_silu(aq))
    for h in range(HEADS):
        sl = slice(h * HD, (h + 1) * HD)
        s = fcol[sl] * hst_ref[0, h] + kcol[sl] * v[:, sl]
        hso_ref[0, h] = s
        o = jnp.sum(qcol[sl] * s, axis=0, keepdims=True)
        ya_ref[0, :, sl] = head_norm(o, gna_ref[...], ag[:, sl])

    cst = cst_ref[0]
    cw = cw_ref[...]
    conv = z[:, OFF_CQKV:OFF_CQKV + C_CONV] * cw[CONV_W - 1:CONV_W]
    for w in range(CONV_W - 1):
        conv = conv + cst[w:w + 1] * cw[w:w + 1]
    act = _silu(conv)
    qc, kc, vc = act[:, 0:HW], act[:, HW:2 * HW], act[:, 2 * HW:3 * HW]
    cg = z[:, OFF_CG:OFF_CG + HW]
    zs = z[:, OFF_CS:OFF_CS + 128]
    beta_all = _sigmoid(zs)
    la_all = -jnp.exp(alog_ref[...]) * _softplus(zs + dtb_ref[...])
    qcol, kcol = col(qc), col(kc)
    for h in range(HEADS):
        sl = slice(h * HD, (h + 1) * HD)
        rq = lax.rsqrt(jnp.sum(qc[:, sl] * qc[:, sl], axis=-1, keepdims=True) + EPS) * (HD ** -0.5)
        rk = lax.rsqrt(jnp.sum(kc[:, sl] * kc[:, sl], axis=-1, keepdims=True) + EPS)
        kh, qh = kcol[sl] * rk, qcol[sl] * rq
        s = dst_ref[0, h] * jnp.exp(la_all[:, HEADS + h:HEADS + h + 1])
        u = beta_all[:, h:h + 1] * (vc[:, sl] - jnp.sum(kh * s, axis=0, keepdims=True))
        s = s + kh * u
        dso_ref[0, h] = s
        o = jnp.sum(qh * s, axis=0, keepdims=True)
        yc_ref[0, :, sl] = head_norm(o, gnc_ref[...], cg[:, sl])


def _sample_rec(lb_p, gna, gnc, cw, alog_p, dtb_p, zs3, hst, dst, cst, layer):
    nb = zs3.shape[0]
    kern = functools.partial(_sample_rec_kernel, layer=layer)
    const = lambda shape: pl.BlockSpec(shape, lambda i: (0,) * len(shape))
    st = pl.BlockSpec((1, HEADS, HD, HD), lambda i: (i, 0, 0, 0))
    yo = pl.BlockSpec((1, 1, HW), lambda i: (i, 0, 0))
    return pl.pallas_call(
        kern,
        grid=(nb,),
        in_specs=[const(lb_p.shape), const((1, HD)), const((1, HD)), const((CONV_W, C_CONV)),
                  const((1, 128)), const((1, 128)),
                  pl.BlockSpec((1, 1, NZ), lambda i: (i, 0, 0)), st, st,
                  pl.BlockSpec((1, CONV_W - 1, C_CONV), lambda i: (i, 0, 0))],
        out_specs=[yo, yo, st, st],
        out_shape=[jax.ShapeDtypeStruct((nb, 1, HW), F32)] * 2
                  + [jax.ShapeDtypeStruct((nb, HEADS, HD, HD), F32)] * 2,
        compiler_params=_params("parallel"),
        name="sample_rec",
    )(lb_p, gna, gnc, cw, alog_p, dtb_p, zs3, hst, dst, cst)


def _paged_kernel(pt_ref, lam_ref, gb_ref, q_ref, kn_ref, vn_ref, *rest, pps, lam_init, b_scale):
    k_refs, v_refs = rest[:pps], rest[pps:2 * pps]
    o_ref, m_sc, l_sc, acc_sc = rest[2 * pps:]
    s_id = pl.program_id(1)
    rows = k_refs[0].shape[0]
    rhead = _iota((8, 128), 0) & (HEADS - 1)
    lane_comp = _div(_iota((8, 128), 1), HD)

    def by_head(r):
        out = jnp.zeros((8, 128), F32)
        for h in range(HEADS):
            out = jnp.where(rhead == h, r[:, h * 128:(h + 1) * 128], out)
        return out

    def merge(x):
        return pltpu.roll(x, HEADS, 0)

    def widen(x, c):
        return jnp.where(lane_comp == c, x, pltpu.roll(x, HD, 1))

    @pl.when(s_id == 0)
    def _():
        m_sc[...] = jnp.full_like(m_sc, -jnp.inf)
        l_sc[...] = jnp.zeros_like(l_sc)
        acc_sc[...] = jnp.zeros_like(acc_sc)

    q8 = by_head(q_ref[0])
    comp_sum = _as01(_div(_iota((128, 128), 0), HD) == _div(_iota((128, 128), 1), HD))

    def scores(i):
        prod = (k_refs[i][...].reshape(rows // 8, 8, 128) * q8).reshape(rows, 128)
        return jnp.dot(prod.astype(BF16), comp_sum, preferred_element_type=F32).reshape(rows // 8, 8, 128)

    def swap(x):
        return pltpu.roll(x, HD, x.ndim - 1)

    def softmax(s3, m_old, l_old):
        mx = jnp.max(s3, axis=0)
        m_new = jnp.maximum(m_old, jnp.maximum(mx, merge(mx)))
        alpha = jnp.exp2(m_old - m_new)
        p = jnp.exp2(s3 - m_new)
        return m_new, alpha * l_old + jnp.sum(p, axis=0), alpha, p

    def weigh(i, alpha, p, acc):
        v3 = v_refs[i][...].reshape(rows // 8, 8, B_DV)
        p_sw = swap(p.reshape(rows, 128)).reshape(rows // 8, 8, 128)
        return [alpha * acc[0] + jnp.sum(p * v3, axis=0),
                swap(alpha) * acc[1] + jnp.sum(p_sw * v3, axis=0)]

    m, l, acc = m_sc[...], l_sc[...], [acc_sc[0], acc_sc[1]]
    s_q, w_q = {}, {}
    for t in range(pps + 2):
        if t < pps:
            s_q[t] = scores(t)
        if 1 <= t <= pps:
            m, l, alpha, pw = softmax(s_q.pop(t - 1), m, l)
            w_q[t - 1] = (alpha, pw)
        if t >= 2:
            acc = weigh(t - 2, *w_q.pop(t - 2), acc)
    m_sc[...] = m
    l_sc[...] = l
    acc_sc[0] = acc[0]
    acc_sc[1] = acc[1]

    @pl.when(s_id == pl.num_programs(1) - 1)
    def _():
        lam = _lambda(lam_ref[...], lam_init)
        qk_new = q8 * by_head(kn_ref[0])
        v_new = by_head(vn_ref[0])
        s_new = jnp.zeros((8, 128), F32)
        for c in range(2):
            s_new = jnp.where(lane_comp == c,
                              jnp.sum(jnp.where(lane_comp == c, qk_new, 0.0), axis=1, keepdims=True), s_new)
        m_old = m_sc[...]
        m_fin = jnp.maximum(m_old, s_new)
        alpha = jnp.exp2(m_old - m_fin)
        p_new = jnp.exp2(s_new - m_fin)
        l_fin = alpha * (l_sc[...] + merge(l_sc[...])) + p_new
        straight = acc_sc[0] + merge(acc_sc[0])
        crossed = acc_sc[1] + merge(acc_sc[1])
        nums = (jnp.where(lane_comp == 0, straight, crossed), jnp.where(lane_comp == 0, crossed, straight))
        outs = []
        for c in range(2):
            a_fin = widen(alpha, c) * nums[c] + widen(p_new, c) * v_new
            outs.append(a_fin / widen(l_fin, c))
        o = outs[0] - lam * outs[1]
        y = o * lax.rsqrt(jnp.mean(o * o, axis=-1, keepdims=True) + EPS) * gb_ref[...] * b_scale
        for h in range(HEADS):
            o_ref[0, :, h * B_DV:(h + 1) * B_DV] = y[h:h + 1]


def _paged_attn(page_table, lam_p, gb, qs3, kn3, vn3, cache_k, cache_v, layer, lam_init, b_scale, pps):
    nb, n_pages = page_table.shape
    depth, n_pool, page, nh, dd = cache_k.shape
    assert nh == HEADS and dd == 128 and cache_v.shape == cache_k.shape
    cache_k = cache_k.reshape(depth, n_pool, page * nh, dd)
    cache_v = cache_v.reshape(depth, n_pool, page * nh, dd)
    kern = functools.partial(_paged_kernel, pps=pps, lam_init=lam_init, b_scale=b_scale)
    row = pl.BlockSpec((1, 1, 2 * HW), lambda b, s, pt: (b, 0, 0))

    def page_spec(i):
        return pl.BlockSpec((None, None, page * nh, dd),
                            lambda b, s, pt: (layer, pt[b, s * pps + i], 0, 0))

    grid_spec = pltpu.PrefetchScalarGridSpec(
        num_scalar_prefetch=1,
        grid=(nb, n_pages // pps),
        in_specs=[pl.BlockSpec(lam_p.shape, lambda b, s, pt: (0, 0)),
                  pl.BlockSpec((1, B_DV), lambda b, s, pt: (0, 0)),
                  row, row, row]
                 + [page_spec(i) for i in range(pps)] * 2,
        out_specs=row,
        scratch_shapes=[pltpu.VMEM((8, 128), F32), pltpu.VMEM((8, 128), F32), pltpu.VMEM((2, 8, B_DV), F32)],
    )
    return pl.pallas_call(
        kern,
        grid_spec=grid_spec,
        out_shape=jax.ShapeDtypeStruct((nb, 1, 2 * HW), F32),
        compiler_params=_params("parallel", "arbitrary"),
        name="paged_attn",
    )(page_table, lam_p, gb, qs3, kn3, vn3, *([cache_k] * pps), *([cache_v] * pps))


def _diag_blocks(s, transpose):
    blocks = jnp.stack([s[:, h * HD:(h + 1) * HD, h * HD:(h + 1) * HD] for h in range(HEADS)], axis=1)
    return jnp.swapaxes(blocks, -1, -2) if transpose else blocks


def kernel(x_prompt, x_sample, cache_k, cache_v, page_table, state_hgrn, state_delta, state_conv,
           meta_tokens, norm_mix, norm_ffn, w_in, w_out, hgrn_lower_bound, hgrn_out_norm,
           diff_q_norm, diff_k_norm, diff_lambda, diff_out_norm, conv_w, delta_a_log,
           delta_dt_bias, delta_out_norm, w_up, w_down):
    nb_p, seq, d_model = x_prompt.shape
    nb_s = x_sample.shape[0]
    depth = w_in.shape[0]
    l_true = N_META + seq
    l_pad = -(-l_true // ABLK) * ABLK
    assert ABLK % BLK == 0 and x_sample.shape[1] == 1
    n_past = page_table.shape[1] * cache_k.shape[2]

    xp = jnp.concatenate([jnp.broadcast_to(meta_tokens[None].astype(F32), (nb_p, N_META, d_model)),
                          x_prompt, jnp.zeros((nb_p, l_pad - l_true, d_model), F32)], axis=1)
    xp = xp.reshape(nb_p * l_pad, d_model)
    xs = x_sample.reshape(nb_s, d_model)

    e256 = jnp.kron(jnp.eye(HEADS, dtype=F32), jnp.ones((HD, HD), F32)).astype(BF16)
    e512 = jnp.kron(jnp.eye(2 * HEADS, dtype=F32), jnp.ones((HD, HD), F32)).astype(BF16)
    tabs_p = _rope_tables(jnp.arange(l_pad))
    tabs_s = _rope_tables(jnp.full((nb_s,), n_past))
    w_in_p = jnp.concatenate([w_in[:, :, 2560:3584], w_in[:, :, 0:2560], w_in[:, :, 3584:D_IN],
                              jnp.zeros(w_in.shape[:2] + (NZ - D_IN,), w_in.dtype)], axis=2).astype(BF16)
    w_out_b, w_up_b, w_down_b = w_out.astype(BF16), w_up.astype(BF16), w_down.astype(BF16)
    lane_pad = lambda a: jnp.pad(a[None, :], ((0, 0), (HEADS, 128 - 2 * HEADS)))

    outs = {k: [] for k in ("kp", "vp", "ks", "vs", "hp", "hs", "dp", "ds", "cp", "cs")}
    for l in range(depth):
        lam_init = 0.8 - 0.6 * math.exp(-0.3 * l)
        b_scale = 1.0 - lam_init
        g_mix, g_ffn = norm_mix[l][None], norm_ffn[l][None]
        gq, gk = jnp.tile(diff_q_norm[l], 2 * HEADS)[None], jnp.tile(diff_k_norm[l], 2 * HEADS)[None]
        gna, gnc = hgrn_out_norm[l][None], delta_out_norm[l][None]
        gna_t, gnc_t = jnp.tile(gna, (1, HEADS)), jnp.tile(gnc, (1, HEADS))
        gb = diff_out_norm[l][None]
        alog_p, dtb_p = lane_pad(delta_a_log[l]), lane_pad(delta_dt_bias[l])

        z3 = _proj(xp, g_mix, w_in_p[l], _row_tile(xp.shape[0], 256)).reshape(nb_p, l_pad, NZ)
        ya, st_a = _hgrn_prompt(hgrn_lower_bound, gna_t, e256, z3, l, l_true)
        qn, kn = _qkrope(z3, gq, gk, e512, tabs_p, _row_tile(l_pad, 384))
        yb = _attn_prompt(diff_lambda[l], gb, qn, kn, z3, lam_init, b_scale)
        yc, st_c = _delta_prompt(conv_w[l], alog_p, dtb_p, gnc_t, e256, z3, l_true)
        rows = nb_p * l_pad
        xp = _merge_ffn(xp, ya.reshape(rows, -1), yb.reshape(rows, -1), yc.reshape(rows, -1),
                        w_out_b[l], g_ffn, w_up_b[l], w_down_b[l], _row_tile(rows, 512), 1024)
        outs["kp"].append(kn[:, :l_true].reshape(nb_p, l_true, HEADS, 2 * HD))
        outs["vp"].append(z3[:, :l_true, OFF_BV:OFF_BV + HEADS * B_DV].reshape(nb_p, l_true, HEADS, B_DV))
        outs["hp"].append(_diag_blocks(st_a, True))
        outs["dp"].append(_diag_blocks(st_c, False))
        outs["cp"].append(z3[:, l_true - (CONV_W - 1):l_true, OFF_CQKV:OFF_CQKV + C_CONV])

        zs = _proj(xs, g_mix, w_in_p[l], nb_s)
        zs3 = zs.reshape(nb_s, 1, NZ)
        ya_s, yc_s, hs_new, ds_new = _sample_rec(hgrn_lower_bound, gna, gnc, conv_w[l], alog_p, dtb_p,
                                                 zs3, state_hgrn[l], state_delta[l], state_conv[l], l)
        qn_s, kn_s = _qkrope(zs.reshape(1, nb_s, NZ), gq, gk, e512, tabs_s, nb_s)
        vn_s = zs[:, OFF_BV:OFF_BV + HEADS * B_DV]
        yb_s = _paged_attn(page_table, diff_lambda[l], gb, qn_s.reshape(nb_s, 1, -1),
                           kn_s.reshape(nb_s, 1, -1), vn_s.reshape(nb_s, 1, -1),
                           cache_k, cache_v, l, lam_init, b_scale, PAGES_PER_STEP)
        xs = _merge_ffn(xs, ya_s.reshape(nb_s, -1), yb_s.reshape(nb_s, -1), yc_s.reshape(nb_s, -1),
                        w_out_b[l], g_ffn, w_up_b[l], w_down_b[l], nb_s, 1024)
        outs["ks"].append(kn_s.reshape(nb_s, 1, HEADS, 2 * HD))
        outs["vs"].append(vn_s.reshape(nb_s, 1, HEADS, B_DV))
        outs["hs"].append(hs_new)
        outs["ds"].append(ds_new)
        outs["cs"].append(jnp.concatenate([state_conv[l][:, 1:], zs[:, None, OFF_CQKV:OFF_CQKV + C_CONV]],
                                          axis=1))

    y_prompt = xp.reshape(nb_p, l_pad, d_model)[:, N_META:l_true]
    y_sample = xs.reshape(nb_s, 1, d_model)
    return (y_prompt, y_sample) + tuple(jnp.stack(outs[k]) for k in
                                        ("kp", "vp", "ks", "vs", "hp", "hs", "dp", "ds", "cp", "cs"))
```
